```python
import functools
import jax, jax.numpy as jnp
from jax import lax
import numpy as np

D_MODEL = 1024
BATCH = 4
SEQ = 8192
DEPTH = 1
DEC_BATCH = 128
DEC_SEQ = 8
PAST_LEN = 16384
PAGE_SIZE = 128

RET_HEADS = 4
RET_DK = 128
RET_DV = 128
RET_WIDTH = RET_HEADS * RET_DV
RET_CHUNK = 128
MLA_HEADS = 8
MLA_NOPE = 64
MLA_ROPE = 32
MLA_V = 64
MLA_WIDTH = MLA_HEADS * MLA_V
Q_LORA = 384
KV_LORA = 256
Q_BLOCK = 128
ROPE_BASE = 10000.0
EPS = 1e-6
N_IN = 2 * RET_HEADS * RET_DK + 2 * RET_WIDTH + Q_LORA + KV_LORA + MLA_ROPE + MLA_WIDTH + 2 * D_MODEL

kernel_name = 'retention_mla_gated_hybrid_step'


def _in_split_points():
    sizes = (RET_HEADS * RET_DK, RET_HEADS * RET_DK, RET_WIDTH, RET_WIDTH,
             Q_LORA, KV_LORA, MLA_ROPE, MLA_WIDTH, D_MODEL, D_MODEL)
    pts, acc = [], 0
    for s in sizes[:-1]:
        acc += s
        pts.append(acc)
    return pts


def rms_norm(x, g):
    xf = x.astype(jnp.float32)
    y = xf * lax.rsqrt(jnp.mean(xf * xf, axis=-1, keepdims=True) + EPS)
    return (y * g.astype(jnp.float32)).astype(x.dtype)


def head_rms(x):
    return x * lax.rsqrt(jnp.mean(x * x, axis=-1, keepdims=True) + EPS)


def rope(x, pos):
    d = x.shape[-1]
    half = d // 2
    inv = jnp.power(ROPE_BASE, -jnp.arange(half, dtype=jnp.float32) * (2.0 / d))
    ang = pos.astype(jnp.float32)[:, None] * inv[None, :]
    cos = jnp.cos(ang)[None, :, None, :]
    sin = jnp.sin(ang)[None, :, None, :]
    xf = x.astype(jnp.float32)
    x1, x2 = xf[..., :half], xf[..., half:]
    return jnp.concatenate([x1 * cos - x2 * sin, x2 * cos + x1 * sin], axis=-1).astype(x.dtype)


def retention_log_decay():
    return jnp.log1p(-jnp.exp2(-5.0 - jnp.arange(RET_HEADS, dtype=jnp.float32)))


def retention_chunk(state, q, k, v):
    L = q.shape[1]
    lg = retention_log_decay()
    idx = jnp.arange(L, dtype=jnp.float32)
    diff = idx[:, None] - idx[None, :]
    decay = jnp.where(diff[None] >= 0, jnp.exp(jnp.maximum(diff, 0.0)[None] * lg[:, None, None]), 0.0)
    qf, kf, vf = q.astype(jnp.float32), k.astype(jnp.float32), v.astype(jnp.float32)
    a = jnp.einsum('bihd,bjhd->bhij', qf, kf) * decay[None]
    inner = jnp.einsum('bhij,bjhv->bihv', a, vf)
    q_decay = jnp.exp((idx[:, None] + 1.0) * lg[None, :])
    cross = jnp.einsum('bihd,bhdv->bihv', qf, state) * q_decay[None, :, :, None]
    k_decay = jnp.exp((L - 1.0 - idx)[:, None] * lg[None, :])
    new_state = (jnp.exp(L * lg)[None, :, None, None] * state
                 + jnp.einsum('bjhd,bjhv->bhdv', kf * k_decay[None, :, :, None], vf))
    return new_state, inner + cross


def retention_prompt(q, k, v):
    B, S, H, _ = q.shape
    C = min(RET_CHUNK, S)
    nc = S // C

    def to_blocks(t):
        return t.reshape(B, nc, C, H, t.shape[-1]).swapaxes(0, 1)

    def step(s, blk):
        s_new, o = retention_chunk(s, blk[0], blk[1], blk[2])
        return s_new, o

    s0 = jnp.zeros((B, H, RET_DK, RET_DV), jnp.float32)
    s_final, o = lax.scan(step, s0, (to_blocks(q), to_blocks(k), to_blocks(v)))
    return o.swapaxes(0, 1).reshape(B, S, H, RET_DV), s_final


def retention_sample(q, k, v, state):
    s_new, o = retention_chunk(state.astype(jnp.float32), q, k, v)
    return o, s_new


def mla_prompt(q_nope, q_pe, c_kv, k_pe, w_uk, w_uv):
    B, S, H, _ = q_nope.shape
    k_nope = jnp.einsum('bsc,chn->bshn', c_kv, w_uk)
    v = jnp.einsum('bsc,chv->bshv', c_kv, w_uv)
    q = jnp.concatenate([q_nope, q_pe], axis=-1)
    k = jnp.concatenate([k_nope, jnp.broadcast_to(k_pe[:, :, None, :], (B, S, H, MLA_ROPE))], axis=-1)
    qb_len = min(Q_BLOCK, S)
    nb = S // qb_len
    qb = q.reshape(B, nb, qb_len, H, q.shape[-1]).swapaxes(0, 1)
    starts = jnp.arange(nb, dtype=jnp.int32) * qb_len
    kpos = jnp.arange(S, dtype=jnp.int32)
    scale = (MLA_NOPE + MLA_ROPE) ** -0.5

    def attend(args):
        qblk, start = args
        s = jnp.einsum('bqhd,bkhd->bhqk', qblk, k).astype(jnp.float32) * scale
        qpos = start + jnp.arange(qb_len, dtype=jnp.int32)
        s = jnp.where(kpos[None, :] <= qpos[:, None], s, -jnp.inf)
        p = jax.nn.softmax(s, axis=-1).astype(v.dtype)
        return jnp.einsum('bhqk,bkhv->bqhv', p, v)

    o = lax.map(attend, (qb, starts))
    return o.swapaxes(0, 1).reshape(B, S, H * MLA_V)


def mla_sample(q_nope, q_pe, c_kv, k_pe, w_uk, w_uv, past_lat, past_pe):
    B, T, H, _ = q_nope.shape
    P = past_lat.shape[1]
    scale = (MLA_NOPE + MLA_ROPE) ** -0.5
    q_lat = jnp.einsum('bthn,chn->bthc', q_nope, w_uk)
    s_past = (jnp.einsum('bthc,bkc->bhtk', q_lat, past_lat)
              + jnp.einsum('bthr,bkr->bhtk', q_pe, past_pe)).astype(jnp.float32)
    s_new = (jnp.einsum('bthc,bkc->bhtk', q_lat, c_kv)
             + jnp.einsum('bthr,bkr->bhtk', q_pe, k_pe)).astype(jnp.float32)
    causal = jnp.tril(jnp.ones((T, T), dtype=bool))
    s_new = jnp.where(causal[None, None], s_new, -jnp.inf)
    p = jax.nn.softmax(jnp.concatenate([s_past, s_new], axis=-1) * scale, axis=-1).astype(c_kv.dtype)
    o_lat = (jnp.einsum('bhtk,bkc->bthc', p[..., :P], past_lat)
             + jnp.einsum('bhtk,bkc->bthc', p[..., P:], c_kv))
    return jnp.einsum('bthc,chv->bthv', o_lat, w_uv).reshape(B, T, H * MLA_V)


def decoder_layer(x, pos, lw, ret_mixer, mla_mixer):
    (norm_g, w_in, q_norm_g, kv_norm_g, w_uq, w_uk, w_uv, w_br_ret, w_br_mla, w_out) = lw
    B, S, _ = x.shape
    h = rms_norm(x, norm_g)
    z = h @ w_in
    q_r, k_r, v_r, g_r, c_q, c_kv, k_pe, g_m, mg_r, mg_m = jnp.split(z, _in_split_points(), axis=-1)
    q_r = rope(q_r.reshape(B, S, RET_HEADS, RET_DK), pos)
    k_r = rope(k_r.reshape(B, S, RET_HEADS, RET_DK), pos) * (RET_DK ** -0.5)
    v_r = v_r.reshape(B, S, RET_HEADS, RET_DV)
    ret_o, ret_state = ret_mixer(q_r, k_r, v_r)
    ret_o = head_rms(ret_o).reshape(B, S, RET_WIDTH).astype(x.dtype) * jax.nn.silu(g_r)
    p_ret = ret_o @ w_br_ret
    c_q = rms_norm(c_q, q_norm_g)
    q = (c_q @ w_uq).reshape(B, S, MLA_HEADS, MLA_NOPE + MLA_ROPE)
    q_nope = q[..., :MLA_NOPE]
    q_pe = rope(q[..., MLA_NOPE:], pos)
    c_kv = rms_norm(c_kv, kv_norm_g)
    k_pe = rope(k_pe[:, :, None, :], pos)[:, :, 0, :]
    mla_o = mla_mixer(q_nope, q_pe, c_kv, k_pe, w_uk, w_uv) * jax.nn.silu(g_m)
    p_mla = mla_o @ w_br_mla
    merged = jax.nn.sigmoid(mg_r) * p_ret + jax.nn.sigmoid(mg_m) * p_mla
    y = x + merged @ w_out
    return y, c_kv, k_pe, ret_state


def setup_inputs(seed: int = 0) -> dict:
    key = jax.random.key(seed)
    ks = jax.random.split(key, 20)
    f32 = jnp.float32
    n_pages = PAST_LEN // PAGE_SIZE
    n_pool = (5 * DEC_BATCH * n_pages) // 4

    def nrm(k, shape, scale):
        return jax.random.normal(k, shape, f32) * scale

    x_prompt = nrm(ks[0], (BATCH, SEQ, D_MODEL), 1.0)
    x_sample = nrm(ks[1], (DEC_BATCH, DEC_SEQ, D_MODEL), 1.0)
    cache_kv_latent = nrm(ks[2], (DEPTH, n_pool, PAGE_SIZE, KV_LORA), 1.0)
    cache_k_rope = nrm(ks[3], (DEPTH, n_pool, PAGE_SIZE, MLA_ROPE), 1.0)
    state_retention = nrm(ks[4], (DEPTH, DEC_BATCH, RET_HEADS, RET_DK, RET_DV), 0.5)
    page_table = jax.random.permutation(ks[5], n_pool)[:DEC_BATCH * n_pages].reshape(DEC_BATCH, n_pages).astype(jnp.int32)
    norm_gain = 1.0 + nrm(ks[6], (DEPTH, D_MODEL), 0.02)
    w_in = nrm(ks[7], (DEPTH, D_MODEL, N_IN), D_MODEL ** -0.5)
    q_norm_gain = 1.0 + nrm(ks[8], (DEPTH, Q_LORA), 0.02)
    kv_norm_gain = 1.0 + nrm(ks[9], (DEPTH, KV_LORA), 0.02)
    w_uq = nrm(ks[10], (DEPTH, Q_LORA, MLA_HEADS * (MLA_NOPE + MLA_ROPE)), Q_LORA ** -0.5)
    w_uk = nrm(ks[11], (DEPTH, KV_LORA, MLA_HEADS, MLA_NOPE), KV_LORA ** -0.5)
    w_uv = nrm(ks[12], (DEPTH, KV_LORA, MLA_HEADS, MLA_V), KV_LORA ** -0.5)
    w_branch_ret = nrm(ks[13], (DEPTH, RET_WIDTH, D_MODEL), RET_WIDTH ** -0.5)
    w_branch_mla = nrm(ks[14], (DEPTH, MLA_WIDTH, D_MODEL), MLA_WIDTH ** -0.5)
    w_out = nrm(ks[15], (DEPTH, D_MODEL, D_MODEL), D_MODEL ** -0.5)
    final_norm_gain = 1.0 + nrm(ks[16], (D_MODEL,), 0.02)
    return {'x_prompt': x_prompt, 'x_sample': x_sample,
            'cache_kv_latent': cache_kv_latent, 'cache_k_rope': cache_k_rope,
            'state_retention': state_retention, 'page_table': page_table,
            'norm_gain': norm_gain, 'w_in': w_in, 'q_norm_gain': q_norm_gain,
            'kv_norm_gain': kv_norm_gain, 'w_uq': w_uq, 'w_uk': w_uk, 'w_uv': w_uv,
            'w_branch_ret': w_branch_ret, 'w_branch_mla': w_branch_mla, 'w_out': w_out,
            'final_norm_gain': final_norm_gain}


def reference(x_prompt, x_sample, cache_kv_latent, cache_k_rope, state_retention, page_table,
              norm_gain, w_in, q_norm_gain, kv_norm_gain, w_uq, w_uk, w_uv,
              w_branch_ret, w_branch_mla, w_out, final_norm_gain):
    dec_b = x_sample.shape[0]
    past_len = page_table.shape[1] * cache_kv_latent.shape[2]
    pos_p = jnp.arange(x_prompt.shape[1], dtype=jnp.int32)
    pos_s = past_len + jnp.arange(x_sample.shape[1], dtype=jnp.int32)
    hp, hs = x_prompt, x_sample
    lat_p, pe_p, st_p, lat_s, pe_s, st_s = [], [], [], [], [], []
    for l in range(DEPTH):
        lw = (norm_gain[l], w_in[l], q_norm_gain[l], kv_norm_gain[l], w_uq[l], w_uk[l], w_uv[l],
              w_branch_ret[l], w_branch_mla[l], w_out[l])
        hp, c_kv, k_pe, st = decoder_layer(hp, pos_p, lw, retention_prompt, mla_prompt)
        lat_p.append(c_kv.astype(cache_kv_latent.dtype))
        pe_p.append(k_pe.astype(cache_k_rope.dtype))
        st_p.append(st.astype(state_retention.dtype))
        past_lat = cache_kv_latent[l, page_table].reshape(dec_b, past_len, KV_LORA)
        past_pe = cache_k_rope[l, page_table].reshape(dec_b, past_len, MLA_ROPE)
        hs, c_kv, k_pe, st = decoder_layer(
            hs, pos_s, lw,
            functools.partial(retention_sample, state=state_retention[l]),
            functools.partial(mla_sample, past_lat=past_lat, past_pe=past_pe))
        lat_s.append(c_kv.astype(cache_kv_latent.dtype))
        pe_s.append(k_pe.astype(cache_k_rope.dtype))
        st_s.append(st.astype(state_retention.dtype))
    y_prompt = rms_norm(hp, final_norm_gain)
    y_sample = rms_norm(hs, final_norm_gain)
    return (y_prompt, y_sample, jnp.stack(lat_p), jnp.stack(pe_p), jnp.stack(st_p),
            jnp.stack(lat_s), jnp.stack(pe_s), jnp.stack(st_s))
```

```python
import functools

import jax
import jax.numpy as jnp
import numpy as np
from jax import lax
from jax.experimental import pallas as pl
from jax.experimental.pallas import tpu as pltpu

F32 = jnp.float32
BF16 = jnp.bfloat16

RET_HEADS = 4
RET_DK = 128
RET_DV = 128
RET_WIDTH = RET_HEADS * RET_DV
RET_CHUNK = 128
MLA_HEADS = 8
MLA_NOPE = 64
MLA_ROPE = 32
MLA_V = 64
MLA_WIDTH = MLA_HEADS * MLA_V
MLA_HEAD_PAD = 128
Q_LORA = 384
KV_LORA = 256
ROPE_BASE = 10000.0
EPS = 1e-6
MLA_SCALE = (MLA_NOPE + MLA_ROPE) ** -0.5

LANES = 128
VMEM_LIMIT_BYTES = 56 * 1024 * 1024

_OFF_QR = 0
_OFF_KR = _OFF_QR + RET_WIDTH
_OFF_VR = _OFF_KR + RET_WIDTH
_OFF_GR = _OFF_VR + RET_WIDTH
_OFF_CQ = _OFF_GR + RET_WIDTH
_OFF_CKV = _OFF_CQ + Q_LORA
_OFF_KPE = _OFF_CKV + KV_LORA
_OFF_KPE_ROT = _OFF_KPE + LANES
_OFF_GM = _OFF_KPE_ROT + LANES
_N_PROJ = _OFF_GM + MLA_WIDTH

_TAB_RC, _TAB_RS, _TAB_QC, _TAB_QS, _TAB_KC, _TAB_KS = (i * LANES for i in range(6))
_TAB_W = 6 * LANES


def _rms(x, g):
    return x * lax.rsqrt(jnp.mean(x * x, axis=-1, keepdims=True) + EPS) * g


def _silu(x):
    return x * jax.nn.sigmoid(x)


def _dot(a, b):
    return jnp.dot(a, b, preferred_element_type=F32)


def _dot_nt(a, b):
    return lax.dot_general(a, b, (((1,), (1,)), ((), ())), preferred_element_type=F32)


def _dot_tn(a, b):
    return lax.dot_general(a, b, (((0,), (0,)), ((), ())), preferred_element_type=F32)


def _proj_kernel(x_ref, tab_ref, ng_ref, wa_ref, qg_ref, kvg_ref, wuq_ref, wkv_ref,
                 qr_ref, kr_ref, vr_ref, sgr_ref, sgm_ref, q_ref, k_ref, v_ref, ckv_ref, kpe_ref):
    h = _rms(x_ref[0], ng_ref[...]).astype(BF16)

    def mm(lo, n):
        return _dot(h, wa_ref[:, lo:lo + n])

    rc = tab_ref[:, _TAB_RC:_TAB_RC + LANES]
    rs = tab_ref[:, _TAB_RS:_TAB_RS + LANES]

    def ret_rope(z):
        return z * rc + pltpu.roll(z, RET_DK // 2, 1) * rs

    zq = mm(_OFF_QR, RET_WIDTH)
    zk = mm(_OFF_KR, RET_WIDTH)
    for i in range(RET_HEADS):
        sl = slice(i * RET_DK, (i + 1) * RET_DK)
        qr_ref[0, :, sl] = ret_rope(zq[:, sl])
        kr_ref[0, :, sl] = ret_rope(zk[:, sl]) * (RET_DK ** -0.5)
    vr_ref[0] = mm(_OFF_VR, RET_WIDTH)
    sgr_ref[0] = _silu(mm(_OFF_GR, RET_WIDTH))
    sgm_ref[0] = _silu(mm(_OFF_GM, MLA_WIDTH))

    cq = _rms(mm(_OFF_CQ, Q_LORA), qg_ref[...]).astype(BF16)
    qq = _dot(cq, wuq_ref[...])
    qc = tab_ref[:, _TAB_QC:_TAB_QC + LANES]
    qs = tab_ref[:, _TAB_QS:_TAB_QS + LANES]
    n_q = MLA_HEADS * MLA_HEAD_PAD
    for i in range(MLA_HEADS):
        sl = slice(i * MLA_HEAD_PAD, (i + 1) * MLA_HEAD_PAD)
        sl_rot = slice(n_q + i * MLA_HEAD_PAD, n_q + (i + 1) * MLA_HEAD_PAD)
        q_ref[0, :, sl] = (qq[:, sl] * qc + qq[:, sl_rot] * qs).astype(BF16)

    ckv = _rms(mm(_OFF_CKV, KV_LORA), kvg_ref[...])
    ckv_ref[0] = ckv
    kv = _dot(ckv.astype(BF16), wkv_ref[...])
    zp = mm(_OFF_KPE, 2 * LANES)
    kc = tab_ref[:, _TAB_KC:_TAB_KC + LANES]
    ks = tab_ref[:, _TAB_KS:_TAB_KS + LANES]
    kpe = zp[:, :LANES] * kc + zp[:, LANES:] * ks
    kpe_ref[0] = kpe[:, MLA_NOPE:MLA_NOPE + MLA_ROPE]
    for i in range(MLA_HEADS):
        sl = slice(i * MLA_HEAD_PAD, (i + 1) * MLA_HEAD_PAD)
        k_ref[0, :, sl] = (kv[:, sl] + kpe).astype(BF16)
        v_ref[0, :, sl] = kv[:, n_q + i * MLA_HEAD_PAD:n_q + (i + 1) * MLA_HEAD_PAD].astype(BF16)


def _proj(x3, tab, w, tm):
    nb, L, D = x3.shape
    n_q = MLA_HEADS * MLA_HEAD_PAD
    tok = lambda n: pl.BlockSpec((1, tm, n), lambda j, b: (b, j, 0))
    full = lambda a: pl.BlockSpec(a.shape, lambda j, b: (0,) * a.ndim)
    outs = [(RET_WIDTH, F32), (RET_WIDTH, F32), (RET_WIDTH, F32), (RET_WIDTH, F32), (MLA_WIDTH, F32),
            (n_q, BF16), (n_q, BF16), (n_q, BF16), (KV_LORA, F32), (MLA_ROPE, F32)]
    return pl.pallas_call(
        _proj_kernel,
        grid=(L // tm, nb),
        in_specs=[tok(D), pl.BlockSpec((tm, _TAB_W), lambda j, b: (j, 0)),
                  full(w["norm_g"]), full(w["w_a"]), full(w["q_g"]), full(w["kv_g"]),
                  full(w["w_uq"]), full(w["w_kv"])],
        out_specs=[tok(n) for n, _ in outs],
        out_shape=[jax.ShapeDtypeStruct((nb, L, n), dt) for n, dt in outs],
        compiler_params=pltpu.CompilerParams(
            dimension_semantics=("arbitrary", "arbitrary"), vmem_limit_bytes=VMEM_LIMIT_BYTES),
        name="proj",
    )(x3, tab, w["norm_g"], w["w_a"], w["q_g"], w["kv_g"], w["w_uq"], w["w_kv"])


def _ret_kernel(q_ref, k_ref, v_ref, sg_ref, st_in_ref, dec_ref, qd_ref, kd_ref, sd_ref,
                o_ref, st_out_ref, st_scr, pad_scr, *, group, rows, chunk):
    c = pl.program_id(1)

    @pl.when(c == 0)
    def _():
        st_scr[...] = st_in_ref[...]
        pad_scr[...] = jnp.zeros_like(pad_scr)

    def operand(ref, g, sl, slot):
        if rows == chunk:
            return ref[g, :, sl]
        pad_scr[slot, 0:rows, :] = ref[g, :, sl]
        return pad_scr[slot]

    for g in range(group):
        for i in range(RET_HEADS):
            sl = slice(i * RET_DK, (i + 1) * RET_DK)
            q = operand(q_ref, g, sl, 0).astype(BF16)
            k = operand(k_ref, g, sl, 1)
            v = operand(v_ref, g, sl, 2).astype(BF16)
            state = st_scr[g, i]
            a = _dot_nt(q, k.astype(BF16)) * dec_ref[i]
            o = _dot(a.astype(BF16), v) + _dot(q, state.astype(BF16)) * qd_ref[i]
            o = o * lax.rsqrt(jnp.mean(o * o, axis=-1, keepdims=True) + EPS)
            o_ref[g, :, sl] = (o[0:rows] * sg_ref[g, :, sl]).astype(BF16)
            kd = (k * kd_ref[i]).astype(BF16)
            st_scr[g, i] = sd_ref[i] * state + _dot_tn(kd, v)

    @pl.when(c == pl.num_programs(1) - 1)
    def _():
        st_out_ref[...] = st_scr[...]


def _ret_tables(rows, chunk):
    lg = jnp.log1p(-jnp.exp2(-5.0 - jnp.arange(RET_HEADS, dtype=F32)))
    idx = jnp.arange(rows, dtype=F32)
    diff = idx[:, None] - idx[None, :]
    decay = jnp.where(diff[None] >= 0, jnp.exp(jnp.maximum(diff, 0.0)[None] * lg[:, None, None]), 0.0)
    q_decay = jnp.exp((idx[None, :] + 1.0) * lg[:, None])
    k_decay = jnp.exp((rows - 1.0 - idx)[None, :] * lg[:, None])
    s_decay = jnp.exp(rows * lg)
    pad = chunk - rows
    decay = jnp.pad(decay, ((0, 0), (0, pad), (0, pad)))
    bcast = lambda t: jnp.broadcast_to(jnp.pad(t, ((0, 0), (0, pad)))[:, :, None], (RET_HEADS, chunk, LANES))
    s_decay = jnp.broadcast_to(s_decay[:, None, None], (RET_HEADS, RET_DK, LANES))
    return decay, bcast(q_decay), bcast(k_decay), s_decay


def _retention(q, k, v, sg, state, group, rows):
    nb, L, _ = q.shape
    chunk = max(rows, RET_CHUNK)
    dec, qd, kd, sd = _ret_tables(rows, chunk)
    tok = pl.BlockSpec((group, rows, RET_WIDTH), lambda b, c: (b, c, 0))
    st = pl.BlockSpec((group, RET_HEADS, RET_DK, RET_DV), lambda b, c: (b, 0, 0, 0))
    full = lambda a: pl.BlockSpec(a.shape, lambda b, c: (0,) * a.ndim)
    return pl.pallas_call(
        functools.partial(_ret_kernel, group=group, rows=rows, chunk=chunk),
        grid=(nb // group, L // rows),
        in_specs=[tok, tok, tok, tok, st, full(dec), full(qd), full(kd), full(sd)],
        out_specs=[tok, st],
        out_shape=[jax.ShapeDtypeStruct((nb, L, RET_WIDTH), BF16),
                   jax.ShapeDtypeStruct(state.shape, F32)],
        scratch_shapes=[pltpu.VMEM((group, RET_HEADS, RET_DK, RET_DV), F32),
                        pltpu.VMEM((3, chunk, RET_DK), F32)],
        compiler_params=pltpu.CompilerParams(
            dimension_semantics=("arbitrary", "arbitrary"), vmem_limit_bytes=VMEM_LIMIT_BYTES),
        name="retention",
    )(q, k, v, sg, state, dec, qd, kd, sd)


def _flash_kernel(qi_ref, ki_ref, q_ref, k_ref, v_ref, sg_ref, o_ref, m_scr, l_scr, acc_scr, *, tile):
    step = pl.program_id(1)
    qi = qi_ref[step]
    ki = ki_ref[step]

    @pl.when(ki == 0)
    def _():
        m_scr[...] = jnp.full_like(m_scr, -jnp.inf)
        l_scr[...] = jnp.zeros_like(l_scr)
        acc_scr[...] = jnp.zeros_like(acc_scr)

    def accumulate(diagonal):
        if diagonal:
            row = lax.broadcasted_iota(jnp.int32, (tile, tile), 0)
            col = lax.broadcasted_iota(jnp.int32, (tile, tile), 1)
            visible = col <= row
        for i in range(MLA_HEADS):
            sl = slice(i * MLA_HEAD_PAD, (i + 1) * MLA_HEAD_PAD)
            s = _dot_nt(q_ref[0, :, sl], k_ref[0, :, sl])
            if diagonal:
                s = jnp.where(visible, s, -jnp.inf)
            m_prev = m_scr[i]
            m_next = jnp.maximum(m_prev, jnp.max(s, axis=1, keepdims=True))
            alpha = jnp.exp(m_prev - m_next)
            p = jnp.exp(s - m_next[:, 0:1])
            l_scr[i] = alpha * l_scr[i] + jnp.sum(p, axis=1, keepdims=True)
            m_scr[i] = m_next
            acc_scr[i] = alpha * acc_scr[i] + _dot(p.astype(BF16), v_ref[0, :, sl])

    @pl.when(ki < qi)
    def _():
        accumulate(False)

    @pl.when(ki == qi)
    def _():
        accumulate(True)
        outs = [(acc_scr[i] / l_scr[i])[:, :MLA_V] for i in range(MLA_HEADS)]
        o_ref[0] = (jnp.concatenate(outs, axis=1) * sg_ref[0]).astype(BF16)


def _flash(q, k, v, sg, tile):
    nb, S, n_q = q.shape
    nt = S // tile
    pairs = [(a, b) for a in range(nt) for b in range(a + 1)]
    qi = jnp.asarray(np.array([p[0] for p in pairs], np.int32))
    ki = jnp.asarray(np.array([p[1] for p in pairs], np.int32))
    qspec = lambda n: pl.BlockSpec((1, tile, n), lambda b, s, qi, ki: (b, qi[s], 0))
    kspec = lambda n: pl.BlockSpec((1, tile, n), lambda b, s, qi, ki: (b, ki[s], 0))
    return pl.pallas_call(
        functools.partial(_flash_kernel, tile=tile),
        grid_spec=pltpu.PrefetchScalarGridSpec(
            num_scalar_prefetch=2,
            grid=(nb, len(pairs)),
            in_specs=[qspec(n_q), kspec(n_q), kspec(n_q), qspec(MLA_WIDTH)],
            out_specs=qspec(MLA_WIDTH),
            scratch_shapes=[pltpu.VMEM((MLA_HEADS, tile, LANES), F32),
                            pltpu.VMEM((MLA_HEADS, tile, LANES), F32),
                            pltpu.VMEM((MLA_HEADS, tile, LANES), F32)]),
        out_shape=jax.ShapeDtypeStruct((nb, S, MLA_WIDTH), BF16),
        compiler_params=pltpu.CompilerParams(
            dimension_semantics=("arbitrary", "arbitrary"), vmem_limit_bytes=VMEM_LIMIT_BYTES),
        name="flash",
    )(qi, ki, q, k, v, sg)


def _decode_kernel(pt_ref, q_ref, knew_ref, ckvnew_ref, sg_ref, wabs_ref, wuv_ref, lat_hbm, pe_hbm,
                   o_ref, lat_buf, pe_buf, sem, qlat_scr, m_scr, l_scr, acc_scr,
                   *, pages, n_steps, page, n_new):
    b = pl.program_id(0)
    j = pl.program_id(1)
    g = b * n_steps + j
    slot = lax.rem(g, 2)
    rows = MLA_HEADS * n_new

    def copies(step, dst_slot):
        out = []
        for p in range(pages):
            pid = pt_ref[step * pages + p]
            out.append(pltpu.make_async_copy(
                lat_hbm.at[pid], lat_buf.at[dst_slot, pl.ds(p * page, page)], sem.at[0, dst_slot]))
            out.append(pltpu.make_async_copy(
                pe_hbm.at[pid], pe_buf.at[dst_slot, pl.ds(p * page, page)], sem.at[1, dst_slot]))
        return out

    @pl.when(g == 0)
    def _():
        for cp in copies(g, slot):
            cp.start()

    @pl.when(g + 1 < pl.num_programs(0) * n_steps)
    def _():
        for cp in copies(g + 1, 1 - slot):
            cp.start()

    @pl.when(j == 0)
    def _():
        for i in range(MLA_HEADS):
            sl = slice(i * MLA_HEAD_PAD, (i + 1) * MLA_HEAD_PAD)
            qlat_scr[i * n_new:(i + 1) * n_new, :] = _dot(q_ref[0, :, sl], wabs_ref[i])
        m_scr[...] = jnp.full_like(m_scr, -jnp.inf)
        l_scr[...] = jnp.zeros_like(l_scr)
        acc_scr[...] = jnp.zeros_like(acc_scr)

    def online(s, values):
        m_prev = m_scr[...]
        m_next = jnp.maximum(m_prev, jnp.max(s, axis=1, keepdims=True))
        alpha = jnp.exp(m_prev - m_next)
        p = jnp.exp(s - m_next[:, 0:1])
        l_scr[...] = alpha * l_scr[...] + jnp.sum(p, axis=1, keepdims=True)
        m_scr[...] = m_next
        acc_scr[...] = jnp.concatenate([alpha, alpha], axis=1) * acc_scr[...] + _dot(p.astype(BF16), values)

    qlat = qlat_scr[...].astype(BF16)
    q_lat = qlat[:, :KV_LORA]
    q_pe = qlat[:, KV_LORA:]

    for cp in copies(g, slot):
        cp.wait()
    lat = lat_buf[slot].astype(BF16)
    pe = pe_buf[slot].astype(BF16)
    online(_dot_nt(q_lat, lat) + _dot_nt(q_pe, pe), lat)

    @pl.when(j == n_steps - 1)
    def _():
        ckv_new = ckvnew_ref[0].astype(BF16)
        k_new = knew_ref[0].astype(BF16)
        s = _dot_nt(q_lat, ckv_new) + _dot_nt(q_pe, k_new)
        t = lax.rem(lax.broadcasted_iota(jnp.int32, s.shape, 0), n_new)
        u = lax.broadcasted_iota(jnp.int32, s.shape, 1)
        online(jnp.where(u <= t, s, -jnp.inf), ckv_new)
        o_lat = (acc_scr[...] / jnp.concatenate([l_scr[...], l_scr[...]], axis=1)).astype(BF16)
        out = jnp.zeros((n_new, MLA_WIDTH), F32)
        for i in range(MLA_HEADS):
            out = out + _dot(o_lat[i * n_new:(i + 1) * n_new, :], wuv_ref[i])
        o_ref[0] = (out * sg_ref[0]).astype(BF16)


def _decode(page_table, q, k_new, ckv_new, sg, w, cache_lat, cache_pe, pages):
    nb, n_new, n_q = q.shape
    n_pool, page, _ = cache_lat.shape
    n_pages = page_table.shape[1]
    n_steps = n_pages // pages
    rows = MLA_HEADS * n_new
    tk = pages * page
    per_b = lambda n0, n1: pl.BlockSpec((1, n0, n1), lambda b, j, pt: (b, 0, 0))
    full = lambda a: pl.BlockSpec(a.shape, lambda b, j, pt: (0,) * a.ndim)
    any_spec = pl.BlockSpec(memory_space=pl.ANY)
    return pl.pallas_call(
        functools.partial(_decode_kernel, pages=pages, n_steps=n_steps, page=page, n_new=n_new),
        grid_spec=pltpu.PrefetchScalarGridSpec(
            num_scalar_prefetch=1,
            grid=(nb, n_steps),
            in_specs=[per_b(n_new, n_q), per_b(LANES, MLA_ROPE), per_b(LANES, KV_LORA), per_b(n_new, MLA_WIDTH),
                      full(w["w_abs"]), full(w["w_uv_pad"]), any_spec, any_spec],
            out_specs=per_b(n_new, MLA_WIDTH),
            scratch_shapes=[pltpu.VMEM((2, tk, KV_LORA), F32),
                            pltpu.VMEM((2, tk, MLA_ROPE), F32),
                            pltpu.SemaphoreType.DMA((2, 2)),
                            pltpu.VMEM((rows, KV_LORA + MLA_ROPE), F32),
                            pltpu.VMEM((rows, LANES), F32),
                            pltpu.VMEM((rows, LANES), F32),
                            pltpu.VMEM((rows, KV_LORA), F32)]),
        out_shape=jax.ShapeDtypeStruct((nb, n_new, MLA_WIDTH), BF16),
        compiler_params=pltpu.CompilerParams(
            dimension_semantics=("arbitrary", "arbitrary"), vmem_limit_bytes=VMEM_LIMIT_BYTES),
        name="decode",
    )(page_table.reshape(-1), q, k_new, ckv_new, sg, w["w_abs"], w["w_uv_pad"], cache_lat, cache_pe)


def _merge_kernel(x_ref, ret_ref, mla_ref, ng_ref, wg_ref, wbr_ref, wbm_ref, wo_ref, fg_ref, y_ref):
    x = x_ref[...]
    D = x.shape[-1]
    h = _rms(x, ng_ref[...]).astype(BF16)
    gates = jax.nn.sigmoid(_dot(h, wg_ref[...]))
    merged = gates[:, :D] * _dot(ret_ref[...], wbr_ref[...]) + gates[:, D:] * _dot(mla_ref[...], wbm_ref[...])
    y = x + _dot(merged.astype(BF16), wo_ref[...])
    y_ref[...] = _rms(y, fg_ref[...])


def _merge(x2, ret, mla, w, tm):
    T, D = x2.shape
    tok = lambda n: pl.BlockSpec((tm, n), lambda i: (i, 0))
    full = lambda a: pl.BlockSpec(a.shape, lambda i: (0,) * a.ndim)
    ws = [w["norm_g"], w["w_gate"], w["w_br_ret"], w["w_br_mla"], w["w_out"], w["final_g"]]
    return pl.pallas_call(
        _merge_kernel,
        grid=(T // tm,),
        in_specs=[tok(D), tok(RET_WIDTH), tok(MLA_WIDTH)] + [full(a) for a in ws],
        out_specs=tok(D),
        out_shape=jax.ShapeDtypeStruct((T, D), F32),
        compiler_params=pltpu.CompilerParams(
            dimension_semantics=("arbitrary",), vmem_limit_bytes=VMEM_LIMIT_BYTES),
        name="merge",
    )(x2, ret, mla, *ws)


def _prep_weights(norm_gain, w_in, q_norm_gain, kv_norm_gain, w_uq, w_uk, w_uv,
                  w_branch_ret, w_branch_mla, w_out, final_norm_gain):
    D = w_in.shape[0]
    sizes = (RET_WIDTH, RET_WIDTH, RET_WIDTH, RET_WIDTH, Q_LORA, KV_LORA, MLA_ROPE, MLA_WIDTH, D, D)
    pts = np.cumsum(sizes)[:-1].tolist()
    wq_r, wk_r, wv_r, wg_r, wc_q, wc_kv, wk_pe, wg_m, wmg_r, wmg_m = jnp.split(w_in, pts, axis=1)
    half = MLA_ROPE // 2
    swap = lambda t: jnp.concatenate([t[..., half:], t[..., :half]], axis=-1)
    place = lambda t: jnp.pad(t, ((0, 0), (MLA_NOPE, MLA_HEAD_PAD - MLA_NOPE - MLA_ROPE)))
    w_a = jnp.concatenate([wq_r, wk_r, wv_r, wg_r, wc_q, wc_kv, place(wk_pe), place(swap(wk_pe)), wg_m], axis=1)
    assert w_a.shape[1] == _N_PROJ

    uq = w_uq.reshape(Q_LORA, MLA_HEADS, MLA_NOPE + MLA_ROPE)
    tail = MLA_HEAD_PAD - MLA_NOPE - MLA_ROPE
    uq_main = jnp.pad(uq, ((0, 0), (0, 0), (0, tail)))
    uq_rot = jnp.pad(swap(uq[..., MLA_NOPE:]), ((0, 0), (0, 0), (MLA_NOPE, tail)))
    w_uq2 = jnp.concatenate([uq_main.reshape(Q_LORA, -1), uq_rot.reshape(Q_LORA, -1)], axis=1)

    uk_pad = jnp.pad(w_uk, ((0, 0), (0, 0), (0, MLA_HEAD_PAD - MLA_NOPE))).reshape(KV_LORA, -1)
    uv_pad = jnp.pad(w_uv, ((0, 0), (0, 0), (0, MLA_HEAD_PAD - MLA_V))).reshape(KV_LORA, -1)
    w_kv = jnp.concatenate([uk_pad, uv_pad], axis=1)

    w_abs = jnp.zeros((MLA_HEADS, MLA_HEAD_PAD, KV_LORA + MLA_ROPE), F32)
    w_abs = w_abs.at[:, :MLA_NOPE, :KV_LORA].set(jnp.transpose(w_uk, (1, 2, 0)))
    w_abs = w_abs.at[:, MLA_NOPE:MLA_NOPE + MLA_ROPE, KV_LORA:].set(jnp.eye(MLA_ROPE, dtype=F32)[None])
    w_uv_pad = jnp.zeros((MLA_HEADS, KV_LORA, MLA_WIDTH), F32)
    for i in range(MLA_HEADS):
        w_uv_pad = w_uv_pad.at[i, :, i * MLA_V:(i + 1) * MLA_V].set(w_uv[:, i, :])

    return dict(
        norm_g=norm_gain.reshape(1, D), q_g=q_norm_gain.reshape(1, Q_LORA), kv_g=kv_norm_gain.reshape(1, KV_LORA),
        final_g=final_norm_gain.reshape(1, D),
        w_a=w_a.astype(BF16), w_uq=w_uq2.astype(BF16), w_kv=w_kv.astype(BF16),
        w_abs=w_abs.astype(BF16), w_uv_pad=w_uv_pad.astype(BF16),
        w_gate=jnp.concatenate([wmg_r, wmg_m], axis=1).astype(BF16),
        w_br_ret=w_branch_ret.astype(BF16), w_br_mla=w_branch_mla.astype(BF16), w_out=w_out.astype(BF16))


def _rope_table(pos):
    posf = pos.astype(F32)[:, None]

    def cos_sin(d):
        inv = jnp.power(ROPE_BASE, -jnp.arange(d // 2, dtype=F32) * (2.0 / d))
        ang = posf * inv[None, :]
        return jnp.cos(ang), jnp.sin(ang)

    rc, rs = cos_sin(RET_DK)
    mc, ms = cos_sin(MLA_ROPE)
    n = pos.shape[0]
    ones = jnp.ones((n, MLA_NOPE), F32)
    zeros = jnp.zeros((n, MLA_NOPE), F32)
    tail = jnp.zeros((n, MLA_HEAD_PAD - MLA_NOPE - MLA_ROPE), F32)
    kc = jnp.concatenate([ones, mc, mc, tail], axis=1)
    ks = jnp.concatenate([zeros, -ms, ms, tail], axis=1)
    return jnp.concatenate([rc, rc, -rs, rs, kc * MLA_SCALE, ks * MLA_SCALE, kc, ks], axis=1)


def kernel(x_prompt, x_sample, cache_kv_latent, cache_k_rope, state_retention, page_table, norm_gain, w_in,
           q_norm_gain, kv_norm_gain, w_uq, w_uk, w_uv, w_branch_ret, w_branch_mla, w_out, final_norm_gain):
    assert norm_gain.shape[0] == 1, "single-layer kernel"
    B, S, D = x_prompt.shape
    NB, T, _ = x_sample.shape
    page = cache_kv_latent.shape[2]
    past_len = page_table.shape[1] * page
    w = _prep_weights(norm_gain[0], w_in[0], q_norm_gain[0], kv_norm_gain[0], w_uq[0], w_uk[0], w_uv[0],
                      w_branch_ret[0], w_branch_mla[0], w_out[0], final_norm_gain)

    tm = min(256, S)
    qr, kr, vr, sgr, sgm, q, k, v, ckv, kpe = _proj(x_prompt, _rope_table(jnp.arange(S, dtype=jnp.int32)), w, tm)
    rows = min(RET_CHUNK, S)
    ret_o, ret_state = _retention(qr, kr, vr, sgr, jnp.zeros((B, RET_HEADS, RET_DK, RET_DV), F32), B, rows)
    mla_o = _flash(q, k, v, sgm, min(512, S))
    y_prompt = _merge(x_prompt.reshape(B * S, D), ret_o.reshape(B * S, RET_WIDTH),
                      mla_o.reshape(B * S, MLA_WIDTH), w, min(512, B * S)).reshape(B, S, D)

    n_tok = NB * T
    tms = min(256, n_tok)
    pos_s = past_len + jnp.arange(T, dtype=jnp.int32)
    tab_s = jnp.tile(_rope_table(pos_s), (tms // T, 1))
    outs = _proj(x_sample.reshape(n_tok // tms, tms, D), tab_s, w, tms)
    qr, kr, vr, sgr, sgm, q, k, v, ckv_s, kpe_s = [o.reshape(NB, T, o.shape[-1]) for o in outs]
    ret_o_s, ret_state_s = _retention(qr, kr, vr, sgr, state_retention[0], 8, T)
    pad_new = lambda t: jnp.pad(t, ((0, 0), (0, LANES - T), (0, 0)))
    mla_o_s = _decode(page_table, q, pad_new(kpe_s), pad_new(ckv_s), sgm, w,
                      cache_kv_latent[0], cache_k_rope[0], pages=16)
    y_sample = _merge(x_sample.reshape(n_tok, D), ret_o_s.reshape(n_tok, RET_WIDTH),
                      mla_o_s.reshape(n_tok, MLA_WIDTH), w, min(512, n_tok)).reshape(NB, T, D)

    return (y_prompt, y_sample, ckv[None], kpe[None], ret_state[None],
            ckv_s[None], kpe_s[None], ret_state_s[None])
```

```python
import functools
import math

import jax
import jax.numpy as jnp
import numpy as np
from jax import lax
from jax.experimental import pallas as pl
from jax.experimental.pallas import tpu as pltpu

F32 = jnp.float32
BF16 = jnp.bfloat16

RET_HEADS = 4
RET_DK = 128
RET_DV = 128
RET_WIDTH = RET_HEADS * RET_DV
RET_CHUNK = 128
MLA_HEADS = 8
MLA_NOPE = 64
MLA_ROPE = 32
MLA_V = 64
MLA_WIDTH = MLA_HEADS * MLA_V
MLA_HEAD_PAD = 128
Q_LORA = 384
KV_LORA = 256
ROPE_BASE = 10000.0
EPS = 1e-6
MLA_Q_SCALE = (MLA_NOPE + MLA_ROPE) ** -0.5 * math.log2(math.e)

LANES = 128
VMEM_LIMIT_BYTES = 56 * 1024 * 1024

_OFF_QR = 0
_OFF_KR = _OFF_QR + RET_WIDTH
_OFF_VR = _OFF_KR + RET_WIDTH
_OFF_GR = _OFF_VR + RET_WIDTH
_OFF_CQ = _OFF_GR + RET_WIDTH
_OFF_CKV = _OFF_CQ + Q_LORA
_OFF_KPE = _OFF_CKV + KV_LORA
_OFF_KPE_ROT = _OFF_KPE + LANES
_N_PROJ = _OFF_KPE_ROT + LANES

_TAB_RC, _TAB_RS, _TAB_QC, _TAB_QS, _TAB_KC, _TAB_KS = (i * LANES for i in range(6))
_TAB_W = 6 * LANES


def _rms(x, g):
    return x * lax.rsqrt(jnp.mean(x * x, axis=-1, keepdims=True) + EPS) * g


def _silu(x):
    return x * jax.nn.sigmoid(x)


def _dot(a, b):
    return jnp.dot(a, b, preferred_element_type=F32)


def _dot_nt(a, b):
    return lax.dot_general(a, b, (((1,), (1,)), ((), ())), preferred_element_type=F32)


def _dot_tn(a, b):
    return lax.dot_general(a, b, (((0,), (0,)), ((), ())), preferred_element_type=F32)


def _proj_kernel(x_ref, tab_ref, ng_ref, wa_ref, wgmt_ref, qg_ref, kvg_ref, wuq_ref, wuk_ref, wuvt_ref,
                 qr_ref, kr_ref, vr_ref, sgr_ref, sgmt_ref, q_ref, k_ref, vt_ref, ckv_ref, kpe_ref):
    h = _rms(x_ref[0], ng_ref[...]).astype(BF16)

    def mm(lo, n):
        return _dot(h, wa_ref[:, lo:lo + n])

    rc = tab_ref[:, _TAB_RC:_TAB_RC + LANES]
    rs = tab_ref[:, _TAB_RS:_TAB_RS + LANES]

    def ret_rope(z):
        return z * rc + pltpu.roll(z, RET_DK // 2, 1) * rs

    zq = mm(_OFF_QR, RET_WIDTH)
    zk = mm(_OFF_KR, RET_WIDTH)
    for i in range(RET_HEADS):
        sl = slice(i * RET_DK, (i + 1) * RET_DK)
        qr_ref[0, :, sl] = ret_rope(zq[:, sl])
        kr_ref[0, :, sl] = ret_rope(zk[:, sl]) * (RET_DK ** -0.5)
    vr_ref[0] = mm(_OFF_VR, RET_WIDTH)
    sgr_ref[0] = _silu(mm(_OFF_GR, RET_WIDTH))
    sgmt = _silu(_dot_nt(wgmt_ref[...], h))
    for i in range(MLA_HEADS):
        sgmt_ref[0, i] = sgmt[i * MLA_V:(i + 1) * MLA_V, :]

    cq = _rms(mm(_OFF_CQ, Q_LORA), qg_ref[...]).astype(BF16)
    qq = _dot(cq, wuq_ref[...])
    qc = tab_ref[:, _TAB_QC:_TAB_QC + LANES]
    qs = tab_ref[:, _TAB_QS:_TAB_QS + LANES]
    n_q = MLA_HEADS * MLA_HEAD_PAD
    for i in range(MLA_HEADS):
        sl = slice(i * MLA_HEAD_PAD, (i + 1) * MLA_HEAD_PAD)
        sl_rot = slice(n_q + i * MLA_HEAD_PAD, n_q + (i + 1) * MLA_HEAD_PAD)
        q_ref[0, i] = (qq[:, sl] * qc + qq[:, sl_rot] * qs).astype(BF16)

    ckv = _rms(mm(_OFF_CKV, KV_LORA), kvg_ref[...])
    ckv_ref[0] = ckv
    ckv_b = ckv.astype(BF16)
    k_nope = _dot(ckv_b, wuk_ref[...])
    zp = mm(_OFF_KPE, 2 * LANES)
    kc = tab_ref[:, _TAB_KC:_TAB_KC + LANES]
    ks = tab_ref[:, _TAB_KS:_TAB_KS + LANES]
    kpe = zp[:, :LANES] * kc + zp[:, LANES:] * ks
    kpe_ref[0] = kpe[:, MLA_NOPE:MLA_NOPE + MLA_ROPE]
    vt = _dot_nt(wuvt_ref[...], ckv_b)
    for i in range(MLA_HEADS):
        sl = slice(i * MLA_HEAD_PAD, (i + 1) * MLA_HEAD_PAD)
        k_ref[0, i] = (k_nope[:, sl] + kpe).astype(BF16)
        vt_ref[0, i] = vt[i * MLA_V:(i + 1) * MLA_V, :].astype(BF16)


def _proj(x3, tab, w, tm):
    nb, L, D = x3.shape
    tok = lambda n: pl.BlockSpec((1, tm, n), lambda j, b: (b, j, 0))
    head_tok = pl.BlockSpec((1, MLA_HEADS, tm, MLA_HEAD_PAD), lambda j, b: (b, 0, j, 0))
    head_feat = pl.BlockSpec((1, MLA_HEADS, MLA_V, tm), lambda j, b: (b, 0, 0, j))
    full = lambda a: pl.BlockSpec(a.shape, lambda j, b: (0,) * a.ndim)
    sds = jax.ShapeDtypeStruct
    tok_out = lambda n: (tok(n), sds((nb, L, n), F32))
    outs = [tok_out(RET_WIDTH), tok_out(RET_WIDTH), tok_out(RET_WIDTH), tok_out(RET_WIDTH),
            (head_feat, sds((nb, MLA_HEADS, MLA_V, L), F32)),
            (head_tok, sds((nb, MLA_HEADS, L, MLA_HEAD_PAD), BF16)),
            (head_tok, sds((nb, MLA_HEADS, L, MLA_HEAD_PAD), BF16)),
            (head_feat, sds((nb, MLA_HEADS, MLA_V, L), BF16)),
            tok_out(KV_LORA), tok_out(MLA_ROPE)]
    ws = [w["norm_g"], w["w_a"], w["w_gmt"], w["q_g"], w["kv_g"], w["w_uq"], w["w_uk"], w["w_uvt"]]
    return pl.pallas_call(
        _proj_kernel,
        grid=(L // tm, nb),
        in_specs=[tok(D), pl.BlockSpec((tm, _TAB_W), lambda j, b: (j, 0))] + [full(a) for a in ws],
        out_specs=[o[0] for o in outs],
        out_shape=[o[1] for o in outs],
        compiler_params=pltpu.CompilerParams(
            dimension_semantics=("arbitrary", "arbitrary"), vmem_limit_bytes=VMEM_LIMIT_BYTES),
        name="proj",
    )(x3, tab, *ws)


def _ret_kernel(q_ref, k_ref, v_ref, sg_ref, st_in_ref, dec_ref, qd_ref, kd_ref, sd_ref,
                o_ref, st_out_ref, st_scr, pad_scr, *, group, rows, chunk):
    c = pl.program_id(1)

    @pl.when(c == 0)
    def _():
        st_scr[...] = st_in_ref[...]
        pad_scr[...] = jnp.zeros_like(pad_scr)

    def operand(ref, g, sl, slot):
        if rows == chunk:
            return ref[g, :, sl]
        pad_scr[slot, 0:rows, :] = ref[g, :, sl]
        return pad_scr[slot]

    for g in range(group):
        for i in range(RET_HEADS):
            sl = slice(i * RET_DK, (i + 1) * RET_DK)
            q = operand(q_ref, g, sl, 0).astype(BF16)
            k = operand(k_ref, g, sl, 1)
            v = operand(v_ref, g, sl, 2).astype(BF16)
            state = st_scr[g, i]
            a = _dot_nt(q, k.astype(BF16)) * dec_ref[i]
            o = _dot(a.astype(BF16), v) + _dot(q, state.astype(BF16)) * qd_ref[i]
            o = o * lax.rsqrt(jnp.mean(o * o, axis=-1, keepdims=True) + EPS)
            o_ref[g, :, sl] = (o[0:rows] * sg_ref[g, :, sl]).astype(BF16)
            kd = (k * kd_ref[i]).astype(BF16)
            st_scr[g, i] = sd_ref[i] * state + _dot_tn(kd, v)

    @pl.when(c == pl.num_programs(1) - 1)
    def _():
        st_out_ref[...] = st_scr[...]


def _ret_tables(rows, chunk):
    lg = jnp.log1p(-jnp.exp2(-5.0 - jnp.arange(RET_HEADS, dtype=F32)))
    idx = jnp.arange(rows, dtype=F32)
    diff = idx[:, None] - idx[None, :]
    decay = jnp.where(diff[None] >= 0, jnp.exp(jnp.maximum(diff, 0.0)[None] * lg[:, None, None]), 0.0)
    q_decay = jnp.exp((idx[None, :] + 1.0) * lg[:, None])
    k_decay = jnp.exp((rows - 1.0 - idx)[None, :] * lg[:, None])
    s_decay = jnp.exp(rows * lg)
    pad = chunk - rows
    decay = jnp.pad(decay, ((0, 0), (0, pad), (0, pad)))
    bcast = lambda t: jnp.broadcast_to(jnp.pad(t, ((0, 0), (0, pad)))[:, :, None], (RET_HEADS, chunk, LANES))
    s_decay = jnp.broadcast_to(s_decay[:, None, None], (RET_HEADS, RET_DK, LANES))
    return decay, bcast(q_decay), bcast(k_decay), s_decay


def _retention(q, k, v, sg, state, group, rows):
    nb, L, _ = q.shape
    chunk = max(rows, RET_CHUNK)
    dec, qd, kd, sd = _ret_tables(rows, chunk)
    tok = pl.BlockSpec((group, rows, RET_WIDTH), lambda b, c: (b, c, 0))
    st = pl.BlockSpec((group, RET_HEADS, RET_DK, RET_DV), lambda b, c: (b, 0, 0, 0))
    full = lambda a: pl.BlockSpec(a.shape, lambda b, c: (0,) * a.ndim)
    return pl.pallas_call(
        functools.partial(_ret_kernel, group=group, rows=rows, chunk=chunk),
        grid=(nb // group, L // rows),
        in_specs=[tok, tok, tok, tok, st, full(dec), full(qd), full(kd), full(sd)],
        out_specs=[tok, st],
        out_shape=[jax.ShapeDtypeStruct((nb, L, RET_WIDTH), BF16),
                   jax.ShapeDtypeStruct(state.shape, F32)],
        scratch_shapes=[pltpu.VMEM((group, RET_HEADS, RET_DK, RET_DV), F32),
                        pltpu.VMEM((3, chunk, RET_DK), F32)],
        compiler_params=pltpu.CompilerParams(
            dimension_semantics=("arbitrary", "arbitrary"), vmem_limit_bytes=VMEM_LIMIT_BYTES),
        name="retention",
    )(q, k, v, sg, state, dec, qd, kd, sd)


def _flash_kernel(qi_ref, ki_ref, q_ref, k_ref, vt_ref, sgt_ref, o_ref, m_scr, l_scr, acc_scr,
                  *, tile, sub_q, sub_k, lookahead, heads_per_iter):
    step = pl.program_id(1)
    qi = qi_ref[step]
    ki = ki_ref[step]

    @pl.when(ki == 0)
    def _():
        m_scr[...] = jnp.full_like(m_scr, -jnp.inf)
        l_scr[...] = jnp.zeros_like(l_scr)
        acc_scr[...] = jnp.zeros_like(acc_scr)

    def accumulate(diagonal):
        units = [(hh, a, b) for hh in range(heads_per_iter)
                 for a in range(tile // sub_q) for b in range(tile // sub_k)
                 if not diagonal or b * sub_k <= (a + 1) * sub_q - 1]

        def head_group(g, carry):
            def scores(hh, a, b):
                i = g * heads_per_iter + hh
                return _dot_nt(k_ref[0, i, pl.ds(b * sub_k, sub_k), :], q_ref[0, i, pl.ds(a * sub_q, sub_q), :])

            ahead = [scores(*u) for u in units[:lookahead]]
            for t, (hh, a, b) in enumerate(units):
                i = g * heads_per_iter + hh
                s = ahead.pop(0)
                if t + lookahead < len(units):
                    ahead.append(scores(*units[t + lookahead]))
                qs = pl.ds(a * sub_q, sub_q)
                if t == 0 or units[t - 1][:2] != (hh, a):
                    m, l, acc = m_scr[i, :, qs], l_scr[i, :, qs], acc_scr[i, :, qs]
                if diagonal and (b + 1) * sub_k - 1 > a * sub_q:
                    key = b * sub_k + lax.broadcasted_iota(jnp.int32, (sub_k, sub_q), 0)
                    query = a * sub_q + lax.broadcasted_iota(jnp.int32, (sub_k, sub_q), 1)
                    s = jnp.where(key <= query, s, -jnp.inf)
                m_next = jnp.maximum(m, jnp.max(s, axis=0, keepdims=True))
                alpha = jnp.exp2(m - m_next)
                p = jnp.exp2(s - m_next).astype(BF16)
                vt1 = jnp.concatenate([vt_ref[0, i, :, pl.ds(b * sub_k, sub_k)], jnp.ones((16, sub_k), BF16)], axis=0)
                pv = _dot(vt1, p)
                l = alpha * l + pv[MLA_V:MLA_V + 1]
                acc = alpha * acc + pv[:MLA_V]
                m = m_next
                if t + 1 == len(units) or units[t + 1][:2] != (hh, a):
                    m_scr[i, :, qs], l_scr[i, :, qs], acc_scr[i, :, qs] = m, l, acc
            return carry

        lax.fori_loop(0, MLA_HEADS // heads_per_iter, head_group, 0)

    @pl.when(ki < qi)
    def _():
        accumulate(False)

    @pl.when(ki == qi)
    def _():
        accumulate(True)
        for i in range(MLA_HEADS):
            o_ref[0, i] = (acc_scr[i] / l_scr[i] * sgt_ref[0, i]).astype(BF16)


FLASH_SUB_Q = 512
FLASH_SUB_K = 512
FLASH_LOOKAHEAD = 2
FLASH_HEADS_PER_ITER = 2


def _flash(q, k, vt, sgt, tile, sub_q, sub_k, lookahead):
    nb, _, S, _ = q.shape
    nt = S // tile
    pairs = [(a, b) for a in range(nt) for b in range(a + 1)]
    qi = jnp.asarray(np.array([p[0] for p in pairs], np.int32))
    ki = jnp.asarray(np.array([p[1] for p in pairs], np.int32))
    q_tok = pl.BlockSpec((1, MLA_HEADS, tile, MLA_HEAD_PAD), lambda b, s, qi, ki: (b, 0, qi[s], 0))
    k_tok = pl.BlockSpec((1, MLA_HEADS, tile, MLA_HEAD_PAD), lambda b, s, qi, ki: (b, 0, ki[s], 0))
    k_feat = pl.BlockSpec((1, MLA_HEADS, MLA_V, tile), lambda b, s, qi, ki: (b, 0, 0, ki[s]))
    q_feat = pl.BlockSpec((1, MLA_HEADS, MLA_V, tile), lambda b, s, qi, ki: (b, 0, 0, qi[s]))
    return pl.pallas_call(
        functools.partial(_flash_kernel, tile=tile, sub_q=sub_q, sub_k=sub_k, lookahead=lookahead,
                          heads_per_iter=FLASH_HEADS_PER_ITER),
        grid_spec=pltpu.PrefetchScalarGridSpec(
            num_scalar_prefetch=2,
            grid=(nb, len(pairs)),
            in_specs=[q_tok, k_tok, k_feat, q_feat],
            out_specs=q_feat,
            scratch_shapes=[pltpu.VMEM((MLA_HEADS, 1, tile), F32),
                            pltpu.VMEM((MLA_HEADS, 1, tile), F32),
                            pltpu.VMEM((MLA_HEADS, MLA_V, tile), F32)]),
        out_shape=jax.ShapeDtypeStruct((nb, MLA_HEADS, MLA_V, S), BF16),
        compiler_params=pltpu.CompilerParams(
            dimension_semantics=("arbitrary", "arbitrary"), vmem_limit_bytes=VMEM_LIMIT_BYTES),
        name="flash",
    )(qi, ki, q, k, vt, sgt)


_DECODE_SLOTS = 3
DECODE_LOOKAHEAD = 3


def _decode_kernel(pt_ref, q_ref, knew_ref, ckvnew_ref, sg_ref, wabs_ref, wuv_ref, lat_hbm, pet_hbm,
                   o_ref, lat_buf, pet_buf, sem, qlat_scr, m_scr, l_scr, acc_scr,
                   *, pages, n_steps, page, n_new, chunk):
    b = pl.program_id(0)
    j = pl.program_id(1)
    n_total = pl.num_programs(0) * n_steps
    g = b * n_steps + j
    rows = MLA_HEADS * n_new

    def copies(step, slot):
        out = []
        for p in range(pages):
            pid = pt_ref[step * pages + p]
            out.append(pltpu.make_async_copy(
                lat_hbm.at[pid], lat_buf.at[slot, pl.ds(p * page, page)], sem.at[slot]))
            out.append(pltpu.make_async_copy(
                pet_hbm.at[pid], pet_buf.at[slot, :, pl.ds(p * page, page)], sem.at[slot]))
        return out

    @pl.when(g == 0)
    def _():
        for cp in copies(0, 0) + copies(1, 1):
            cp.start()

    @pl.when(j == 0)
    def _():
        qf = q_ref[0].astype(F32)
        for i in range(MLA_HEADS):
            qlat_scr[i * n_new:(i + 1) * n_new, :] = _dot(qf[i * n_new:(i + 1) * n_new].astype(BF16), wabs_ref[i])
        m_scr[...] = jnp.full_like(m_scr, -jnp.inf)
        l_scr[...] = jnp.zeros_like(l_scr)
        acc_scr[...] = jnp.zeros_like(acc_scr)

    slot = lax.rem(g, _DECODE_SLOTS)
    for cp in copies(g, slot):
        cp.wait()
    gather = copies(jnp.minimum(g + 2, n_total - 1), lax.rem(g + 2, _DECODE_SLOTS))
    n_chunks = pages * page // chunk
    per_chunk = -(-len(gather) // n_chunks)

    qlat = qlat_scr[...].astype(BF16)
    q_lat = qlat[:, :KV_LORA]
    q_pe = qlat[:, KV_LORA:]

    def online(carry, s, values):
        m, l, acc = carry
        m_next = jnp.maximum(m, jnp.max(s, axis=1, keepdims=True))
        alpha = jnp.exp2(m - m_next)
        p = jnp.exp2(s - m_next[:, 0:1])
        l = alpha * l + jnp.sum(p, axis=1, keepdims=True)
        acc = jnp.concatenate([alpha, alpha], axis=1) * acc + _dot(p.astype(BF16), values)
        return m_next, l, acc

    def chunk_scores(c):
        lat = lat_buf[slot, pl.ds(c * chunk, chunk), :].astype(BF16)
        pet = pet_buf[slot, :, pl.ds(c * chunk, chunk)].astype(BF16)
        return lat, _dot_nt(q_lat, lat) + _dot(q_pe, pet)

    carry = (m_scr[...], l_scr[...], acc_scr[...])
    ahead = [chunk_scores(c) for c in range(min(DECODE_LOOKAHEAD, n_chunks))]
    for c in range(n_chunks):
        lat, s = ahead.pop(0)
        if c + DECODE_LOOKAHEAD < n_chunks:
            ahead.append(chunk_scores(c + DECODE_LOOKAHEAD))
        for cp in gather[c * per_chunk:(c + 1) * per_chunk]:
            cp.start()
        carry = online(carry, s, lat)
    m_scr[...], l_scr[...], acc_scr[...] = carry

    @pl.when(j == n_steps - 1)
    def _():
        ckv_new = ckvnew_ref[0].astype(BF16)
        k_new = knew_ref[0].astype(BF16)
        s = _dot_nt(q_lat, ckv_new) + _dot_nt(q_pe, k_new)
        t = lax.rem(lax.broadcasted_iota(jnp.int32, s.shape, 0), n_new)
        u = lax.broadcasted_iota(jnp.int32, s.shape, 1)
        m, l, acc = online(carry, jnp.where(u <= t, s, -jnp.inf), ckv_new)
        o_lat = (acc / jnp.concatenate([l, l], axis=1)).astype(BF16)
        y = _dot(o_lat, wuv_ref[...])
        row_head = lax.broadcasted_iota(jnp.int32, y.shape, 0) // n_new
        col_head = lax.broadcasted_iota(jnp.int32, y.shape, 1) // MLA_V
        y = jnp.where(row_head == col_head, y, 0.0)
        out = y[0:n_new]
        for i in range(1, MLA_HEADS):
            out = out + y[i * n_new:(i + 1) * n_new]
        o_ref[0] = (out * sg_ref[0]).astype(BF16)

    @pl.when(g == n_total - 1)
    def _():
        for ahead in (1, 2):
            for cp in copies(g, lax.rem(g + ahead, _DECODE_SLOTS)):
                cp.wait()


def _decode(page_table, q, k_new, ckv_new, sg, w, cache_lat, cache_pet, pages, chunk):
    nb, rows, _ = q.shape
    n_new = rows // MLA_HEADS
    n_pool, page, _ = cache_lat.shape
    n_pages = page_table.shape[1]
    n_steps = n_pages // pages
    assert nb * n_steps >= _DECODE_SLOTS
    tk = pages * page
    per_b = lambda n0, n1: pl.BlockSpec((1, n0, n1), lambda b, j, pt: (b, 0, 0))
    full = lambda a: pl.BlockSpec(a.shape, lambda b, j, pt: (0,) * a.ndim)
    any_spec = pl.BlockSpec(memory_space=pl.ANY)
    return pl.pallas_call(
        functools.partial(_decode_kernel, pages=pages, n_steps=n_steps, page=page, n_new=n_new, chunk=chunk),
        grid_spec=pltpu.PrefetchScalarGridSpec(
            num_scalar_prefetch=1,
            grid=(nb, n_steps),
            in_specs=[per_b(rows, MLA_HEAD_PAD), per_b(LANES, MLA_ROPE), per_b(LANES, KV_LORA),
                      per_b(n_new, MLA_WIDTH), full(w["w_abs"]), full(w["w_uv"]), any_spec, any_spec],
            out_specs=per_b(n_new, MLA_WIDTH),
            scratch_shapes=[pltpu.VMEM((_DECODE_SLOTS, tk, KV_LORA), F32),
                            pltpu.VMEM((_DECODE_SLOTS, MLA_ROPE, tk), F32),
                            pltpu.SemaphoreType.DMA((_DECODE_SLOTS,)),
                            pltpu.VMEM((rows, KV_LORA + MLA_ROPE), F32),
                            pltpu.VMEM((rows, LANES), F32),
                            pltpu.VMEM((rows, LANES), F32),
                            pltpu.VMEM((rows, KV_LORA), F32)]),
        out_shape=jax.ShapeDtypeStruct((nb, n_new, MLA_WIDTH), BF16),
        compiler_params=pltpu.CompilerParams(
            dimension_semantics=("arbitrary", "arbitrary"), vmem_limit_bytes=VMEM_LIMIT_BYTES),
        name="decode",
    )(page_table.reshape(-1), q, k_new, ckv_new, sg, w["w_abs"], w["w_uv"], cache_lat, cache_pet)


def _merge_kernel(x_ref, ret_ref, mla_ref, ng_ref, wg_ref, wbr_ref, wbm_ref, wo_ref, fg_ref, y_ref,
                  *, mla_feature_major):
    x = x_ref[0]
    D = x.shape[-1]
    h = _rms(x, ng_ref[...]).astype(BF16)
    gates = jax.nn.sigmoid(_dot(h, wg_ref[...]))
    p_mla = _dot_tn(mla_ref[0], wbm_ref[...]) if mla_feature_major else _dot(mla_ref[0], wbm_ref[...])
    merged = gates[:, :D] * _dot(ret_ref[0], wbr_ref[...]) + gates[:, D:] * p_mla
    y = x + _dot(merged.astype(BF16), wo_ref[...])
    y_ref[0] = _rms(y, fg_ref[...])


def _merge(x3, ret, mla, w, tm, mla_feature_major):
    nb, L, D = x3.shape
    tok = lambda n: pl.BlockSpec((1, tm, n), lambda b, j: (b, j, 0))
    feat = pl.BlockSpec((1, MLA_WIDTH, tm), lambda b, j: (b, 0, j))
    full = lambda a: pl.BlockSpec(a.shape, lambda b, j: (0,) * a.ndim)
    ws = [w["norm_g"], w["w_gate"], w["w_br_ret"], w["w_br_mla"], w["w_out"], w["final_g"]]
    return pl.pallas_call(
        functools.partial(_merge_kernel, mla_feature_major=mla_feature_major),
        grid=(nb, L // tm),
        in_specs=[tok(D), tok(RET_WIDTH), feat if mla_feature_major else tok(MLA_WIDTH)] + [full(a) for a in ws],
        out_specs=tok(D),
        out_shape=jax.ShapeDtypeStruct((nb, L, D), F32),
        compiler_params=pltpu.CompilerParams(
            dimension_semantics=("arbitrary", "arbitrary"), vmem_limit_bytes=VMEM_LIMIT_BYTES),
        name="merge",
    )(x3, ret, mla, *ws)


def _prep_weights(norm_gain, w_in, q_norm_gain, kv_norm_gain, w_uq, w_uk, w_uv,
                  w_branch_ret, w_branch_mla, w_out, final_norm_gain):
    D = w_in.shape[0]
    sizes = (RET_WIDTH, RET_WIDTH, RET_WIDTH, RET_WIDTH, Q_LORA, KV_LORA, MLA_ROPE, MLA_WIDTH, D, D)
    pts = np.cumsum(sizes)[:-1].tolist()
    wq_r, wk_r, wv_r, wg_r, wc_q, wc_kv, wk_pe, wg_m, wmg_r, wmg_m = jnp.split(w_in, pts, axis=1)
    half = MLA_ROPE // 2
    swap = lambda t: jnp.concatenate([t[..., half:], t[..., :half]], axis=-1)
    place = lambda t: jnp.pad(t, ((0, 0), (MLA_NOPE, MLA_HEAD_PAD - MLA_NOPE - MLA_ROPE)))
    w_a = jnp.concatenate([wq_r, wk_r, wv_r, wg_r, wc_q, wc_kv, place(wk_pe), place(swap(wk_pe))], axis=1)
    assert w_a.shape[1] == _N_PROJ

    uq = w_uq.reshape(Q_LORA, MLA_HEADS, MLA_NOPE + MLA_ROPE)
    tail = MLA_HEAD_PAD - MLA_NOPE - MLA_ROPE
    uq_main = jnp.pad(uq, ((0, 0), (0, 0), (0, tail)))
    uq_rot = jnp.pad(swap(uq[..., MLA_NOPE:]), ((0, 0), (0, 0), (MLA_NOPE, tail)))
    w_uq2 = jnp.concatenate([uq_main.reshape(Q_LORA, -1), uq_rot.reshape(Q_LORA, -1)], axis=1)
    uk_pad = jnp.pad(w_uk, ((0, 0), (0, 0), (0, MLA_HEAD_PAD - MLA_NOPE))).reshape(KV_LORA, -1)
    uv = w_uv.reshape(KV_LORA, MLA_WIDTH)

    w_abs = jnp.zeros((MLA_HEADS, MLA_HEAD_PAD, KV_LORA + MLA_ROPE), F32)
    w_abs = w_abs.at[:, :MLA_NOPE, :KV_LORA].set(jnp.transpose(w_uk, (1, 2, 0)))
    w_abs = w_abs.at[:, MLA_NOPE:MLA_NOPE + MLA_ROPE, KV_LORA:].set(jnp.eye(MLA_ROPE, dtype=F32)[None])

    return dict(
        norm_g=norm_gain.reshape(1, D), q_g=q_norm_gain.reshape(1, Q_LORA), kv_g=kv_norm_gain.reshape(1, KV_LORA),
        final_g=final_norm_gain.reshape(1, D),
        w_a=w_a.astype(BF16), w_gmt=wg_m.T.astype(BF16), w_uq=w_uq2.astype(BF16), w_uk=uk_pad.astype(BF16),
        w_uvt=uv.T.astype(BF16), w_uv=uv.astype(BF16), w_abs=w_abs.astype(BF16),
        w_gate=jnp.concatenate([wmg_r, wmg_m], axis=1).astype(BF16),
        w_br_ret=w_branch_ret.astype(BF16), w_br_mla=w_branch_mla.astype(BF16), w_out=w_out.astype(BF16))


def _rope_table(pos):
    posf = pos.astype(F32)[:, None]

    def cos_sin(d):
        inv = jnp.power(ROPE_BASE, -jnp.arange(d // 2, dtype=F32) * (2.0 / d))
        ang = posf * inv[None, :]
        return jnp.cos(ang), jnp.sin(ang)

    rc, rs = cos_sin(RET_DK)
    mc, ms = cos_sin(MLA_ROPE)
    n = pos.shape[0]
    ones = jnp.ones((n, MLA_NOPE), F32)
    zeros = jnp.zeros((n, MLA_NOPE), F32)
    tail = jnp.zeros((n, MLA_HEAD_PAD - MLA_NOPE - MLA_ROPE), F32)
    kc = jnp.concatenate([ones, mc, mc, tail], axis=1)
    ks = jnp.concatenate([zeros, -ms, ms, tail], axis=1)
    return jnp.concatenate([rc, rc, -rs, rs, kc * MLA_Q_SCALE, ks * MLA_Q_SCALE, kc, ks], axis=1)


def kernel(x_prompt, x_sample, cache_kv_latent, cache_k_rope, state_retention, page_table, norm_gain, w_in,
           q_norm_gain, kv_norm_gain, w_uq, w_uk, w_uv, w_branch_ret, w_branch_mla, w_out, final_norm_gain):
    assert norm_gain.shape[0] == 1, "single-layer kernel"
    B, S, D = x_prompt.shape
    NB, T, _ = x_sample.shape
    page = cache_kv_latent.shape[2]
    n_pages = page_table.shape[1]
    past_len = n_pages * page
    w = _prep_weights(norm_gain[0], w_in[0], q_norm_gain[0], kv_norm_gain[0], w_uq[0], w_uk[0], w_uv[0],
                      w_branch_ret[0], w_branch_mla[0], w_out[0], final_norm_gain)

    tm = min(256, S)
    qr, kr, vr, sgr, sgmt, q, k, vt, ckv, kpe = _proj(x_prompt, _rope_table(jnp.arange(S, dtype=jnp.int32)), w, tm)
    rows = min(RET_CHUNK, S)
    ret_o, ret_state = _retention(qr, kr, vr, sgr, jnp.zeros((B, RET_HEADS, RET_DK, RET_DV), F32), B, rows)
    tile = min(1024, S)
    mla_t = _flash(q, k, vt, sgmt, tile, min(FLASH_SUB_Q, tile), min(FLASH_SUB_K, tile), FLASH_LOOKAHEAD)
    y_prompt = _merge(x_prompt, ret_o, mla_t.reshape(B, MLA_WIDTH, S), w, min(512, S), True)

    n_tok = NB * T
    tms = min(256, n_tok)
    nbs = n_tok // tms
    pos_s = past_len + jnp.arange(T, dtype=jnp.int32)
    tab_s = jnp.tile(_rope_table(pos_s), (tms // T, 1))
    qr, kr, vr, sgr, sgmt, q, _, _, ckv_s, kpe_s = _proj(x_sample.reshape(nbs, tms, D), tab_s, w, tms)
    per_seq = lambda t: t.reshape(NB, T, t.shape[-1])
    qr, kr, vr, sgr, ckv_s, kpe_s = map(per_seq, (qr, kr, vr, sgr, ckv_s, kpe_s))
    ret_o_s, ret_state_s = _retention(qr, kr, vr, sgr, state_retention[0], 8, T)
    q_s = q.reshape(nbs, MLA_HEADS, tms // T, T, MLA_HEAD_PAD).transpose(0, 2, 1, 3, 4)
    q_s = q_s.reshape(NB, MLA_HEADS * T, MLA_HEAD_PAD)
    sgm_s = sgmt.reshape(nbs, MLA_WIDTH, tms).transpose(0, 2, 1).reshape(NB, T, MLA_WIDTH)
    pad_new = lambda t: jnp.pad(t, ((0, 0), (0, LANES - T), (0, 0)))
    decode_pages = math.gcd(n_pages, 32)
    mla_o_s = _decode(page_table, q_s, pad_new(kpe_s), pad_new(ckv_s), sgm_s, w,
                      cache_kv_latent[0], jnp.swapaxes(cache_k_rope[0], 1, 2),
                      pages=decode_pages, chunk=min(1024, decode_pages * page))
    y_sample = _merge(x_sample.reshape(1, n_tok, D), ret_o_s.reshape(1, n_tok, RET_WIDTH),
                      mla_o_s.reshape(1, n_tok, MLA_WIDTH), w, min(512, n_tok), False).reshape(NB, T, D)

    return (y_prompt, y_sample, ckv[None], kpe[None], ret_state[None],
            ckv_s[None], kpe_s[None], ret_state_s[None])
```

```python
import functools
import math

import jax
import jax.numpy as jnp
import numpy as np
from jax import lax
from jax.experimental import pallas as pl
from jax.experimental.pallas import tpu as pltpu

F32 = jnp.float32
BF16 = jnp.bfloat16

RET_HEADS = 4
RET_DK = 128
RET_DV = 128
RET_WIDTH = RET_HEADS * RET_DV
RET_CHUNK = 128
MLA_HEADS = 8
MLA_NOPE = 64
MLA_ROPE = 32
MLA_V = 64
MLA_WIDTH = MLA_HEADS * MLA_V
MLA_HEAD_PAD = 128
Q_LORA = 384
KV_LORA = 256
ROPE_BASE = 10000.0
EPS = 1e-6
MLA_Q_SCALE = (MLA_NOPE + MLA_ROPE) ** -0.5 * math.log2(math.e)

LANES = 128
VMEM_LIMIT_BYTES = 56 * 1024 * 1024

_OFF_QR = 0
_OFF_KR = _OFF_QR + RET_WIDTH
_OFF_VR = _OFF_KR + RET_WIDTH
_OFF_GR = _OFF_VR + RET_WIDTH
_OFF_CQ = _OFF_GR + RET_WIDTH
_OFF_KPE = _OFF_CQ + Q_LORA
_OFF_CKV = _OFF_KPE + LANES
_N_PROJ = _OFF_CKV + KV_LORA

_TAB_RC, _TAB_RS, _TAB_MC, _TAB_MS_DOWN, _TAB_MS_UP = (i * LANES for i in range(5))
_TAB_W = 5 * LANES


def _rms(x, g):
    return x * lax.rsqrt(jnp.mean(x * x, axis=-1, keepdims=True) + EPS) * g


def _silu(x):
    return x * jax.nn.sigmoid(x)


def _dot(a, b):
    return jnp.dot(a, b, preferred_element_type=F32)


def _dot_nt(a, b):
    return lax.dot_general(a, b, (((1,), (1,)), ((), ())), preferred_element_type=F32)


def _dot_tn(a, b):
    return lax.dot_general(a, b, (((0,), (0,)), ((), ())), preferred_element_type=F32)


def _proj_kernel(x_ref, tab_ref, ng_ref, wa_ref, wgmt_ref, qg_ref, kvg_ref, wuq_ref, wuk_ref, wuvt_ref,
                 qr_ref, kr_ref, vr_ref, sgr_ref, sgmt_ref, q_ref, k_ref, vt_ref, ckv_ref, kpe_ref):
    h = _rms(x_ref[0], ng_ref[...]).astype(BF16)

    def mm(lo, n):
        return _dot(h, wa_ref[:, lo:lo + n])

    z_lat = mm(_OFF_CQ, Q_LORA + LANES)
    z_ckv = mm(_OFF_CKV, KV_LORA)
    zq = mm(_OFF_QR, RET_WIDTH)
    zk = mm(_OFF_KR, RET_WIDTH)
    zv = mm(_OFF_VR, RET_WIDTH)
    zg = mm(_OFF_GR, RET_WIDTH)
    zgm = _dot_nt(wgmt_ref[...], h)
    cq = _rms(z_lat[:, :Q_LORA], qg_ref[...]).astype(BF16)
    qq = _dot(cq, wuq_ref[...])
    ckv = _rms(z_ckv, kvg_ref[...])
    ckv_b = ckv.astype(BF16)
    k_nope = _dot(ckv_b, wuk_ref[...])
    vt = _dot_nt(wuvt_ref[...], ckv_b)

    rc = tab_ref[:, _TAB_RC:_TAB_RC + LANES]
    rs = tab_ref[:, _TAB_RS:_TAB_RS + LANES]
    mc = tab_ref[:, _TAB_MC:_TAB_MC + LANES]
    ms_down = tab_ref[:, _TAB_MS_DOWN:_TAB_MS_DOWN + LANES]
    ms_up = tab_ref[:, _TAB_MS_UP:_TAB_MS_UP + LANES]

    def ret_rope(z):
        return z * rc + pltpu.roll(z, RET_DK // 2, 1) * rs

    def mla_rope(z):
        return (z * mc + pltpu.roll(z, LANES - MLA_ROPE // 2, 1) * ms_down
                + pltpu.roll(z, MLA_ROPE // 2, 1) * ms_up)

    for i in range(RET_HEADS):
        sl = slice(i * RET_DK, (i + 1) * RET_DK)
        qr_ref[0, :, sl] = ret_rope(zq[:, sl]).astype(qr_ref.dtype)
        kr_ref[0, :, sl] = ret_rope(zk[:, sl]) * (RET_DK ** -0.5)
    vr_ref[0] = zv.astype(vr_ref.dtype)
    sgr_ref[0] = _silu(zg)
    sgmt = _silu(zgm)
    ckv_ref[0] = ckv
    kpe = mla_rope(z_lat[:, Q_LORA:])
    kpe_ref[0] = kpe[:, MLA_NOPE:MLA_NOPE + MLA_ROPE]
    for i in range(MLA_HEADS):
        sl = slice(i * MLA_HEAD_PAD, (i + 1) * MLA_HEAD_PAD)
        sgmt_ref[0, i] = sgmt[i * MLA_V:(i + 1) * MLA_V, :]
        q_ref[0, i] = (mla_rope(qq[:, sl]) * MLA_Q_SCALE).astype(BF16)
        k_ref[0, i] = (k_nope[:, sl] + kpe).astype(BF16)
        vt_ref[0, i] = vt[i * MLA_V:(i + 1) * MLA_V, :].astype(BF16)


def _proj(x3, tab, w, tm, ret_dtype):
    nb, L, D = x3.shape
    tok = lambda n: pl.BlockSpec((1, tm, n), lambda j, b: (b, j, 0))
    head_tok = pl.BlockSpec((1, MLA_HEADS, tm, MLA_HEAD_PAD), lambda j, b: (b, 0, j, 0))
    head_feat = pl.BlockSpec((1, MLA_HEADS, MLA_V, tm), lambda j, b: (b, 0, 0, j))
    full = lambda a: pl.BlockSpec(a.shape, lambda j, b: (0,) * a.ndim)
    sds = jax.ShapeDtypeStruct
    tok_out = lambda n, dt=F32: (tok(n), sds((nb, L, n), dt))
    outs = [tok_out(RET_WIDTH, ret_dtype), tok_out(RET_WIDTH), tok_out(RET_WIDTH, ret_dtype), tok_out(RET_WIDTH),
            (head_feat, sds((nb, MLA_HEADS, MLA_V, L), F32)),
            (head_tok, sds((nb, MLA_HEADS, L, MLA_HEAD_PAD), BF16)),
            (head_tok, sds((nb, MLA_HEADS, L, MLA_HEAD_PAD), BF16)),
            (head_feat, sds((nb, MLA_HEADS, MLA_V, L), BF16)),
            tok_out(KV_LORA), tok_out(MLA_ROPE)]
    ws = [w["norm_g"], w["w_a"], w["w_gmt"], w["q_g"], w["kv_g"], w["w_uq"], w["w_uk"], w["w_uvt"]]
    return pl.pallas_call(
        _proj_kernel,
        grid=(L // tm, nb),
        in_specs=[tok(D), pl.BlockSpec((tm, _TAB_W), lambda j, b: (j, 0))] + [full(a) for a in ws],
        out_specs=[o[0] for o in outs],
        out_shape=[o[1] for o in outs],
        compiler_params=pltpu.CompilerParams(
            dimension_semantics=("arbitrary", "arbitrary"), vmem_limit_bytes=VMEM_LIMIT_BYTES),
        name="proj",
    )(x3, tab, *ws)


def _ret_kernel(q_ref, k_ref, v_ref, sg_ref, st_in_ref, dec_ref, qd_ref, kd_ref, sd_ref,
                o_ref, st_out_ref, st_scr, pad_scr, *, group, rows, chunk):
    c = pl.program_id(1)

    @pl.when(c == 0)
    def _():
        st_scr[...] = st_in_ref[...]
        pad_scr[...] = jnp.zeros_like(pad_scr)

    units = [(g, i) for g in range(group) for i in range(RET_HEADS)]

    def operand(ref, g, sl, slot):
        if rows == chunk:
            return ref[g, :, sl]
        idx = (g * RET_HEADS + sl.start // RET_DK) * 3 + slot
        pad_scr[idx, 0:rows, :] = ref[g, :, sl].astype(F32)
        return pad_scr[idx]

    def products(g, i):
        sl = slice(i * RET_DK, (i + 1) * RET_DK)
        q = operand(q_ref, g, sl, 0).astype(BF16)
        k = operand(k_ref, g, sl, 1)
        v = operand(v_ref, g, sl, 2).astype(BF16)
        state = st_scr[g, i]
        return q, k, v, state, _dot_nt(q, k.astype(BF16)), _dot(q, state.astype(BF16))

    ahead = [products(*u) for u in units[:RET_LOOKAHEAD]]
    for t, (g, i) in enumerate(units):
        sl = slice(i * RET_DK, (i + 1) * RET_DK)
        q, k, v, state, a, cross = ahead.pop(0)
        if t + RET_LOOKAHEAD < len(units):
            ahead.append(products(*units[t + RET_LOOKAHEAD]))
        o = _dot((a * dec_ref[i]).astype(BF16), v) + cross * qd_ref[i]
        o = o * lax.rsqrt(jnp.mean(o * o, axis=-1, keepdims=True) + EPS)
        o_ref[g, :, sl] = (o[0:rows] * sg_ref[g, :, sl]).astype(BF16)
        kd = (k * kd_ref[i]).astype(BF16)
        st_scr[g, i] = sd_ref[i] * state + _dot_tn(kd, v)

    @pl.when(c == pl.num_programs(1) - 1)
    def _():
        st_out_ref[...] = st_scr[...]


RET_LOOKAHEAD = 3


def _ret_tables(rows, chunk):
    lg = jnp.log1p(-jnp.exp2(-5.0 - jnp.arange(RET_HEADS, dtype=F32)))
    idx = jnp.arange(rows, dtype=F32)
    diff = idx[:, None] - idx[None, :]
    decay = jnp.where(diff[None] >= 0, jnp.exp(jnp.maximum(diff, 0.0)[None] * lg[:, None, None]), 0.0)
    q_decay = jnp.exp((idx[None, :] + 1.0) * lg[:, None])
    k_decay = jnp.exp((rows - 1.0 - idx)[None, :] * lg[:, None])
    s_decay = jnp.exp(rows * lg)
    pad = chunk - rows
    decay = jnp.pad(decay, ((0, 0), (0, pad), (0, pad)))
    bcast = lambda t: jnp.broadcast_to(jnp.pad(t, ((0, 0), (0, pad)))[:, :, None], (RET_HEADS, chunk, LANES))
    s_decay = jnp.broadcast_to(s_decay[:, None, None], (RET_HEADS, RET_DK, LANES))
    return decay, bcast(q_decay), bcast(k_decay), s_decay


def _retention(q, k, v, sg, state, group, rows):
    nb, L, _ = q.shape
    chunk = max(rows, RET_CHUNK)
    dec, qd, kd, sd = _ret_tables(rows, chunk)
    tok = pl.BlockSpec((group, rows, RET_WIDTH), lambda b, c: (b, c, 0))
    st = pl.BlockSpec((group, RET_HEADS, RET_DK, RET_DV), lambda b, c: (b, 0, 0, 0))
    full = lambda a: pl.BlockSpec(a.shape, lambda b, c: (0,) * a.ndim)
    return pl.pallas_call(
        functools.partial(_ret_kernel, group=group, rows=rows, chunk=chunk),
        grid=(nb // group, L // rows),
        in_specs=[tok, tok, tok, tok, st, full(dec), full(qd), full(kd), full(sd)],
        out_specs=[tok, st],
        out_shape=[jax.ShapeDtypeStruct((nb, L, RET_WIDTH), BF16),
                   jax.ShapeDtypeStruct(state.shape, F32)],
        scratch_shapes=[pltpu.VMEM((group, RET_HEADS, RET_DK, RET_DV), F32),
                        pltpu.VMEM((3 * group * RET_HEADS if rows < chunk else 3, chunk, RET_DK), F32)],
        compiler_params=pltpu.CompilerParams(
            dimension_semantics=("arbitrary", "arbitrary"), vmem_limit_bytes=VMEM_LIMIT_BYTES),
        name="retention",
    )(q, k, v, sg, state, dec, qd, kd, sd)


def _flash_kernel(qi_ref, ki_ref, q_ref, k_ref, vt_ref, sgt_ref, o_ref, m_scr, l_scr, acc_scr,
                  *, tile, sub_q, sub_k, lookahead, heads_per_iter):
    step = pl.program_id(1)
    qi = qi_ref[step]
    ki = ki_ref[step]

    @pl.when(ki == 0)
    def _():
        m_scr[...] = jnp.full_like(m_scr, -jnp.inf)
        l_scr[...] = jnp.zeros_like(l_scr)
        acc_scr[...] = jnp.zeros_like(acc_scr)

    def accumulate(diagonal):
        units = [(hh, a, b) for hh in range(heads_per_iter)
                 for a in range(tile // sub_q) for b in range(tile // sub_k)
                 if not diagonal or b * sub_k <= (a + 1) * sub_q - 1]

        def head_group(g, carry):
            def scores(hh, a, b):
                i = g * heads_per_iter + hh
                return _dot_nt(k_ref[0, i, pl.ds(b * sub_k, sub_k), :], q_ref[0, i, pl.ds(a * sub_q, sub_q), :])

            ahead = [scores(*u) for u in units[:lookahead]]
            for t, (hh, a, b) in enumerate(units):
                i = g * heads_per_iter + hh
                s = ahead.pop(0)
                if t + lookahead < len(units):
                    ahead.append(scores(*units[t + lookahead]))
                qs = pl.ds(a * sub_q, sub_q)
                if t == 0 or units[t - 1][:2] != (hh, a):
                    m, l, acc = m_scr[i, :, qs], l_scr[i, :, qs], acc_scr[i, :, qs]
                if diagonal and (b + 1) * sub_k - 1 > a * sub_q:
                    key = b * sub_k + lax.broadcasted_iota(jnp.int32, (sub_k, sub_q), 0)
                    query = a * sub_q + lax.broadcasted_iota(jnp.int32, (sub_k, sub_q), 1)
                    s = jnp.where(key <= query, s, -jnp.inf)
                m_next = jnp.maximum(m, jnp.max(s, axis=0, keepdims=True))
                alpha = jnp.exp2(m - m_next)
                p = jnp.exp2(s - m_next).astype(BF16)
                vt1 = jnp.concatenate([vt_ref[0, i, :, pl.ds(b * sub_k, sub_k)], jnp.ones((16, sub_k), BF16)], axis=0)
                pv = _dot(vt1, p)
                l = alpha * l + pv[MLA_V:MLA_V + 1]
                acc = alpha * acc + pv[:MLA_V]
                m = m_next
                if t + 1 == len(units) or units[t + 1][:2] != (hh, a):
                    m_scr[i, :, qs], l_scr[i, :, qs], acc_scr[i, :, qs] = m, l, acc
            return carry

        lax.fori_loop(0, MLA_HEADS // heads_per_iter, head_group, 0)

    @pl.when(ki < qi)
    def _():
        accumulate(False)

    @pl.when(ki == qi)
    def _():
        accumulate(True)
        for i in range(MLA_HEADS):
            o_ref[0, i] = (acc_scr[i] / l_scr[i] * sgt_ref[0, i]).astype(BF16)


FLASH_SUB_Q = 512
FLASH_SUB_K = 512
FLASH_LOOKAHEAD = 2
FLASH_HEADS_PER_ITER = 4


def _flash(q, k, vt, sgt, tile, sub_q, sub_k, lookahead):
    nb, _, S, _ = q.shape
    nt = S // tile
    pairs = [(a, b) for a in range(nt) for b in range(a + 1)]
    qi = jnp.asarray(np.array([p[0] for p in pairs], np.int32))
    ki = jnp.asarray(np.array([p[1] for p in pairs], np.int32))
    q_tok = pl.BlockSpec((1, MLA_HEADS, tile, MLA_HEAD_PAD), lambda b, s, qi, ki: (b, 0, qi[s], 0))
    k_tok = pl.BlockSpec((1, MLA_HEADS, tile, MLA_HEAD_PAD), lambda b, s, qi, ki: (b, 0, ki[s], 0))
    k_feat = pl.BlockSpec((1, MLA_HEADS, MLA_V, tile), lambda b, s, qi, ki: (b, 0, 0, ki[s]))
    q_feat = pl.BlockSpec((1, MLA_HEADS, MLA_V, tile), lambda b, s, qi, ki: (b, 0, 0, qi[s]))
    return pl.pallas_call(
        functools.partial(_flash_kernel, tile=tile, sub_q=sub_q, sub_k=sub_k, lookahead=lookahead,
                          heads_per_iter=FLASH_HEADS_PER_ITER),
        grid_spec=pltpu.PrefetchScalarGridSpec(
            num_scalar_prefetch=2,
            grid=(nb, len(pairs)),
            in_specs=[q_tok, k_tok, k_feat, q_feat],
            out_specs=q_feat,
            scratch_shapes=[pltpu.VMEM((MLA_HEADS, 1, tile), F32),
                            pltpu.VMEM((MLA_HEADS, 1, tile), F32),
                            pltpu.VMEM((MLA_HEADS, MLA_V, tile), F32)]),
        out_shape=jax.ShapeDtypeStruct((nb, MLA_HEADS, MLA_V, S), BF16),
        compiler_params=pltpu.CompilerParams(
            dimension_semantics=("arbitrary", "arbitrary"), vmem_limit_bytes=VMEM_LIMIT_BYTES),
        name="flash",
    )(qi, ki, q, k, vt, sgt)


_DECODE_SLOTS = 3
DECODE_PAGES = 64
DECODE_LOOKAHEAD = 3


def _decode_kernel(pt_ref, q_ref, knew_ref, ckvnew_ref, sg_ref, wabs_ref, wuv_ref, lat_hbm, pet_hbm,
                   o_ref, lat_buf, pet_buf, sem, qlat_scr, m_scr, l_scr, acc_scr,
                   *, pages, n_steps, page, n_new, chunk):
    b = pl.program_id(0)
    j = pl.program_id(1)
    n_total = pl.num_programs(0) * n_steps
    g = b * n_steps + j
    rows = MLA_HEADS * n_new

    def copies(step, slot):
        out = []
        for p in range(pages):
            pid = pt_ref[step * pages + p]
            out.append(pltpu.make_async_copy(
                lat_hbm.at[pid], lat_buf.at[slot, pl.ds(p * page, page)], sem.at[slot]))
            out.append(pltpu.make_async_copy(
                pet_hbm.at[pid], pet_buf.at[slot, :, pl.ds(p * page, page)], sem.at[slot]))
        return out

    @pl.when(g == 0)
    def _():
        for cp in copies(0, 0) + copies(1, 1):
            cp.start()

    @pl.when(j == 0)
    def _():
        qf = q_ref[0].astype(F32)
        for i in range(MLA_HEADS):
            qlat_scr[i * n_new:(i + 1) * n_new, :] = _dot(qf[i * n_new:(i + 1) * n_new].astype(BF16), wabs_ref[i])
        m_scr[...] = jnp.full_like(m_scr, -jnp.inf)
        l_scr[...] = jnp.zeros_like(l_scr)
        acc_scr[...] = jnp.zeros_like(acc_scr)

    slot = lax.rem(g, _DECODE_SLOTS)
    for cp in copies(g, slot):
        cp.wait()
    gather = copies(jnp.minimum(g + 2, n_total - 1), lax.rem(g + 2, _DECODE_SLOTS))
    n_chunks = pages * page // chunk
    per_chunk = -(-len(gather) // n_chunks)

    qlat = qlat_scr[...].astype(BF16)
    q_lat = qlat[:, :KV_LORA]
    q_pe = qlat[:, KV_LORA:]

    def online(carry, s, values):
        m, l, acc = carry
        m_next = jnp.maximum(m, jnp.max(s, axis=1, keepdims=True))
        alpha = jnp.exp2(m - m_next)
        p = jnp.exp2(s - m_next[:, 0:1])
        l = alpha * l + jnp.sum(p, axis=1, keepdims=True)
        acc = jnp.concatenate([alpha, alpha], axis=1) * acc + _dot(p.astype(BF16), values)
        return m_next, l, acc

    def chunk_scores(c):
        lat = lat_buf[slot, pl.ds(c * chunk, chunk), :].astype(BF16)
        pet = pet_buf[slot, :, pl.ds(c * chunk, chunk)].astype(BF16)
        return lat, _dot_nt(q_lat, lat) + _dot(q_pe, pet)

    carry = (m_scr[...], l_scr[...], acc_scr[...])
    ahead = [chunk_scores(c) for c in range(min(DECODE_LOOKAHEAD, n_chunks))]
    for c in range(n_chunks):
        lat, s = ahead.pop(0)
        if c + DECODE_LOOKAHEAD < n_chunks:
            ahead.append(chunk_scores(c + DECODE_LOOKAHEAD))
        for cp in gather[c * per_chunk:(c + 1) * per_chunk]:
            cp.start()
        carry = online(carry, s, lat)
    m_scr[...], l_scr[...], acc_scr[...] = carry

    @pl.when(j == n_steps - 1)
    def _():
        ckv_new = ckvnew_ref[0].astype(BF16)
        k_new = knew_ref[0].astype(BF16)
        s = _dot_nt(q_lat, ckv_new) + _dot_nt(q_pe, k_new)
        t = lax.rem(lax.broadcasted_iota(jnp.int32, s.shape, 0), n_new)
        u = lax.broadcasted_iota(jnp.int32, s.shape, 1)
        m, l, acc = online(carry, jnp.where(u <= t, s, -jnp.inf), ckv_new)
        o_lat = (acc / jnp.concatenate([l, l], axis=1)).astype(BF16)
        y = _dot(o_lat, wuv_ref[...])
        row_head = lax.broadcasted_iota(jnp.int32, y.shape, 0) // n_new
        col_head = lax.broadcasted_iota(jnp.int32, y.shape, 1) // MLA_V
        y = jnp.where(row_head == col_head, y, 0.0)
        out = y[0:n_new]
        for i in range(1, MLA_HEADS):
            out = out + y[i * n_new:(i + 1) * n_new]
        o_ref[0] = (out * sg_ref[0]).astype(BF16)

    @pl.when(g == n_total - 1)
    def _():
        for ahead in (1, 2):
            for cp in copies(g, lax.rem(g + ahead, _DECODE_SLOTS)):
                cp.wait()


def _decode(page_table, q, k_new, ckv_new, sg, w, cache_lat, cache_pet, pages, chunk):
    nb, rows, _ = q.shape
    n_new = rows // MLA_HEADS
    n_pool, page, _ = cache_lat.shape
    n_pages = page_table.shape[1]
    n_steps = n_pages // pages
    assert nb * n_steps >= _DECODE_SLOTS
    tk = pages * page
    per_b = lambda n0, n1: pl.BlockSpec((1, n0, n1), lambda b, j, pt: (b, 0, 0))
    full = lambda a: pl.BlockSpec(a.shape, lambda b, j, pt: (0,) * a.ndim)
    any_spec = pl.BlockSpec(memory_space=pl.ANY)
    return pl.pallas_call(
        functools.partial(_decode_kernel, pages=pages, n_steps=n_steps, page=page, n_new=n_new, chunk=chunk),
        grid_spec=pltpu.PrefetchScalarGridSpec(
            num_scalar_prefetch=1,
            grid=(nb, n_steps),
            in_specs=[per_b(rows, MLA_HEAD_PAD), per_b(LANES, MLA_ROPE), per_b(LANES, KV_LORA),
                      per_b(n_new, MLA_WIDTH), full(w["w_abs"]), full(w["w_uv"]), any_spec, any_spec],
            out_specs=per_b(n_new, MLA_WIDTH),
            scratch_shapes=[pltpu.VMEM((_DECODE_SLOTS, tk, KV_LORA), F32),
                            pltpu.VMEM((_DECODE_SLOTS, MLA_ROPE, tk), F32),
                            pltpu.SemaphoreType.DMA((_DECODE_SLOTS,)),
                            pltpu.VMEM((rows, KV_LORA + MLA_ROPE), F32),
                            pltpu.VMEM((rows, LANES), F32),
                            pltpu.VMEM((rows, LANES), F32),
                            pltpu.VMEM((rows, KV_LORA), F32)]),
        out_shape=jax.ShapeDtypeStruct((nb, n_new, MLA_WIDTH), BF16),
        compiler_params=pltpu.CompilerParams(
            dimension_semantics=("arbitrary", "arbitrary"), vmem_limit_bytes=VMEM_LIMIT_BYTES),
        name="decode",
    )(page_table.reshape(-1), q, k_new, ckv_new, sg, w["w_abs"], w["w_uv"], cache_lat, cache_pet)


def _merge_kernel(x_ref, ret_ref, mla_ref, ng_ref, wg_ref, wbr_ref, wbm_ref, wo_ref, fg_ref, y_ref,
                  *, mla_feature_major):
    x = x_ref[0]
    D = x.shape[-1]
    h = _rms(x, ng_ref[...]).astype(BF16)
    gates = jax.nn.sigmoid(_dot(h, wg_ref[...]))
    p_mla = _dot_tn(mla_ref[0], wbm_ref[...]) if mla_feature_major else _dot(mla_ref[0], wbm_ref[...])
    merged = gates[:, :D] * _dot(ret_ref[0], wbr_ref[...]) + gates[:, D:] * p_mla
    y = x + _dot(merged.astype(BF16), wo_ref[...])
    y_ref[0] = _rms(y, fg_ref[...])


def _merge(x3, ret, mla, w, tm, mla_feature_major):
    nb, L, D = x3.shape
    tok = lambda n: pl.BlockSpec((1, tm, n), lambda b, j: (b, j, 0))
    feat = pl.BlockSpec((1, MLA_WIDTH, tm), lambda b, j: (b, 0, j))
    full = lambda a: pl.BlockSpec(a.shape, lambda b, j: (0,) * a.ndim)
    ws = [w["norm_g"], w["w_gate"], w["w_br_ret"], w["w_br_mla"], w["w_out"], w["final_g"]]
    return pl.pallas_call(
        functools.partial(_merge_kernel, mla_feature_major=mla_feature_major),
        grid=(nb, L // tm),
        in_specs=[tok(D), tok(RET_WIDTH), feat if mla_feature_major else tok(MLA_WIDTH)] + [full(a) for a in ws],
        out_specs=tok(D),
        out_shape=jax.ShapeDtypeStruct((nb, L, D), F32),
        compiler_params=pltpu.CompilerParams(
            dimension_semantics=("arbitrary", "arbitrary"), vmem_limit_bytes=VMEM_LIMIT_BYTES),
        name="merge",
    )(x3, ret, mla, *ws)


def _prep_weights(norm_gain, w_in, q_norm_gain, kv_norm_gain, w_uq, w_uk, w_uv,
                  w_branch_ret, w_branch_mla, w_out, final_norm_gain):
    D = w_in.shape[0]
    sizes = (RET_WIDTH, RET_WIDTH, RET_WIDTH, RET_WIDTH, Q_LORA, KV_LORA, MLA_ROPE, MLA_WIDTH, D, D)
    pts = np.cumsum(sizes)[:-1].tolist()
    wq_r, wk_r, wv_r, wg_r, wc_q, wc_kv, wk_pe, wg_m, wmg_r, wmg_m = jnp.split(w_in, pts, axis=1)
    place = lambda t: jnp.pad(t, ((0, 0), (MLA_NOPE, MLA_HEAD_PAD - MLA_NOPE - MLA_ROPE)))
    w_a = jnp.concatenate([wq_r, wk_r, wv_r, wg_r, wc_q, place(wk_pe), wc_kv], axis=1)
    assert w_a.shape[1] == _N_PROJ

    uq = w_uq.reshape(Q_LORA, MLA_HEADS, MLA_NOPE + MLA_ROPE)
    tail = MLA_HEAD_PAD - MLA_NOPE - MLA_ROPE
    w_uq2 = jnp.pad(uq, ((0, 0), (0, 0), (0, tail))).reshape(Q_LORA, -1)
    uk_pad = jnp.pad(w_uk, ((0, 0), (0, 0), (0, MLA_HEAD_PAD - MLA_NOPE))).reshape(KV_LORA, -1)
    uv = w_uv.reshape(KV_LORA, MLA_WIDTH)

    w_abs = jnp.zeros((MLA_HEADS, MLA_HEAD_PAD, KV_LORA + MLA_ROPE), F32)
    w_abs = w_abs.at[:, :MLA_NOPE, :KV_LORA].set(jnp.transpose(w_uk, (1, 2, 0)))
    w_abs = w_abs.at[:, MLA_NOPE:MLA_NOPE + MLA_ROPE, KV_LORA:].set(jnp.eye(MLA_ROPE, dtype=F32)[None])

    return dict(
        norm_g=norm_gain.reshape(1, D), q_g=q_norm_gain.reshape(1, Q_LORA), kv_g=kv_norm_gain.reshape(1, KV_LORA),
        final_g=final_norm_gain.reshape(1, D),
        w_a=w_a.astype(BF16), w_gmt=wg_m.T.astype(BF16), w_uq=w_uq2.astype(BF16), w_uk=uk_pad.astype(BF16),
        w_uvt=uv.T.astype(BF16), w_uv=uv.astype(BF16), w_abs=w_abs.astype(BF16),
        w_gate=jnp.concatenate([wmg_r, wmg_m], axis=1).astype(BF16),
        w_br_ret=w_branch_ret.astype(BF16), w_br_mla=w_branch_mla.astype(BF16), w_out=w_out.astype(BF16))


def _rope_table(pos):
    posf = pos.astype(F32)[:, None]

    def cos_sin(d):
        inv = jnp.power(ROPE_BASE, -jnp.arange(d // 2, dtype=F32) * (2.0 / d))
        ang = posf * inv[None, :]
        return jnp.cos(ang), jnp.sin(ang)

    rc, rs = cos_sin(RET_DK)
    mc, ms = cos_sin(MLA_ROPE)
    n = pos.shape[0]
    ones = jnp.ones((n, MLA_NOPE), F32)
    zeros = lambda width: jnp.zeros((n, width), F32)
    half = MLA_ROPE // 2
    tail = MLA_HEAD_PAD - MLA_NOPE - MLA_ROPE
    m_cos = jnp.concatenate([ones, mc, mc, zeros(tail)], axis=1)
    m_sin_down = jnp.concatenate([zeros(MLA_NOPE), -ms, zeros(half + tail)], axis=1)
    m_sin_up = jnp.concatenate([zeros(MLA_NOPE + half), ms, zeros(tail)], axis=1)
    return jnp.concatenate([rc, rc, -rs, rs, m_cos, m_sin_down, m_sin_up], axis=1)


def kernel(x_prompt, x_sample, cache_kv_latent, cache_k_rope, state_retention, page_table, norm_gain, w_in,
           q_norm_gain, kv_norm_gain, w_uq, w_uk, w_uv, w_branch_ret, w_branch_mla, w_out, final_norm_gain):
    assert norm_gain.shape[0] == 1, "single-layer kernel"
    B, S, D = x_prompt.shape
    NB, T, _ = x_sample.shape
    page = cache_kv_latent.shape[2]
    n_pages = page_table.shape[1]
    past_len = n_pages * page
    w = _prep_weights(norm_gain[0], w_in[0], q_norm_gain[0], kv_norm_gain[0], w_uq[0], w_uk[0], w_uv[0],
                      w_branch_ret[0], w_branch_mla[0], w_out[0], final_norm_gain)

    tm = min(256, S)
    qr, kr, vr, sgr, sgmt, q, k, vt, ckv, kpe = _proj(x_prompt, _rope_table(jnp.arange(S, dtype=jnp.int32)), w, tm, BF16)
    rows = min(RET_CHUNK, S)
    ret_o, ret_state = _retention(qr, kr, vr, sgr, jnp.zeros((B, RET_HEADS, RET_DK, RET_DV), F32), B, rows)
    tile = min(1024, S)
    mla_t = _flash(q, k, vt, sgmt, tile, min(FLASH_SUB_Q, tile), min(FLASH_SUB_K, tile), FLASH_LOOKAHEAD)
    y_prompt = _merge(x_prompt, ret_o, mla_t.reshape(B, MLA_WIDTH, S), w, min(512, S), True)

    n_tok = NB * T
    tms = min(256, n_tok)
    nbs = n_tok // tms
    pos_s = past_len + jnp.arange(T, dtype=jnp.int32)
    tab_s = jnp.tile(_rope_table(pos_s), (tms // T, 1))
    qr, kr, vr, sgr, sgmt, q, _, _, ckv_s, kpe_s = _proj(x_sample.reshape(nbs, tms, D), tab_s, w, tms, F32)
    per_seq = lambda t: t.reshape(NB, T, t.shape[-1])
    qr, kr, vr, sgr, ckv_s, kpe_s = map(per_seq, (qr, kr, vr, sgr, ckv_s, kpe_s))
    ret_o_s, ret_state_s = _retention(qr, kr, vr, sgr, state_retention[0], 8, T)
    q_s = q.reshape(nbs, MLA_HEADS, tms // T, T, MLA_HEAD_PAD).transpose(0, 2, 1, 3, 4)
    q_s = q_s.reshape(NB, MLA_HEADS * T, MLA_HEAD_PAD)
    sgm_s = sgmt.reshape(nbs, MLA_WIDTH, tms).transpose(0, 2, 1).reshape(NB, T, MLA_WIDTH)
    pad_new = lambda t: jnp.pad(t, ((0, 0), (0, LANES - T), (0, 0)))
    decode_pages = math.gcd(n_pages, DECODE_PAGES)
    mla_o_s = _decode(page_table, q_s, pad_new(kpe_s), pad_new(ckv_s), sgm_s, w,
                      cache_kv_latent[0], jnp.swapaxes(cache_k_rope[0], 1, 2),
                      pages=decode_pages, chunk=min(1024, decode_pages * page))
    y_sample = _merge(x_sample.reshape(1, n_tok, D), ret_o_s.reshape(1, n_tok, RET_WIDTH),
                      mla_o_s.reshape(1, n_tok, MLA_WIDTH), w, min(512, n_tok), False).reshape(NB, T, D)

    return (y_prompt, y_sample, ckv[None], kpe[None], ret_state[None],
            ckv_s[None], kpe_s[None], ret_state_s[None])
```

```python
import functools
import math

import jax
import jax.numpy as jnp
import numpy as np
from jax import lax
from jax.experimental import pallas as pl
from jax.experimental.pallas import tpu as pltpu

F32 = jnp.float32
BF16 = jnp.bfloat16

RET_HEADS = 4
RET_DK = 128
RET_DV = 128
RET_WIDTH = RET_HEADS * RET_DV
RET_CHUNK = 128
MLA_HEADS = 8
MLA_NOPE = 64
MLA_ROPE = 32
MLA_V = 64
MLA_WIDTH = MLA_HEADS * MLA_V
MLA_HEAD_PAD = 128
Q_LORA = 384
KV_LORA = 256
ROPE_BASE = 10000.0
EPS = 1e-6
MLA_Q_SCALE = (MLA_NOPE + MLA_ROPE) ** -0.5 * math.log2(math.e)

LANES = 128
VMEM_LIMIT_BYTES = 56 * 1024 * 1024

_OFF_QR = 0
_OFF_KR = _OFF_QR + RET_WIDTH
_OFF_VR = _OFF_KR + RET_WIDTH
_OFF_GR = _OFF_VR + RET_WIDTH
_OFF_CQ = _OFF_GR + RET_WIDTH
_OFF_KPE = _OFF_CQ + Q_LORA
_OFF_CKV = _OFF_KPE + LANES
_N_PROJ = _OFF_CKV + KV_LORA

_TAB_RC, _TAB_RS, _TAB_MC, _TAB_MS_DOWN, _TAB_MS_UP = (i * LANES for i in range(5))
_TAB_W = 5 * LANES


def _rms(x, g):
    return x * lax.rsqrt(jnp.mean(x * x, axis=-1, keepdims=True) + EPS) * g


def _silu(x):
    return x * jax.nn.sigmoid(x)


def _dot(a, b):
    return jnp.dot(a, b, preferred_element_type=F32)


def _dot_nt(a, b):
    return lax.dot_general(a, b, (((1,), (1,)), ((), ())), preferred_element_type=F32)


def _dot_tn(a, b):
    return lax.dot_general(a, b, (((0,), (0,)), ((), ())), preferred_element_type=F32)


def _proj_kernel(x_ref, tab_ref, ng_ref, wa_ref, wgmt_ref, qg_ref, kvg_ref, wuq_ref, wuk_ref, wuvt_ref, hsel_ref,
                 qr_ref, kr_ref, vr_ref, sgr_ref, sgmt_ref, q_ref, k_ref, vt_ref, ckv_ref, kpe_ref, qn2_ref, kn2_ref):
    h = _rms(x_ref[0], ng_ref[...]).astype(BF16)

    def mm(lo, n):
        return _dot(h, wa_ref[:, lo:lo + n])

    z_lat = mm(_OFF_CQ, Q_LORA + LANES)
    z_ckv = mm(_OFF_CKV, KV_LORA)
    zq = mm(_OFF_QR, RET_WIDTH)
    zk = mm(_OFF_KR, RET_WIDTH)
    zv = mm(_OFF_VR, RET_WIDTH)
    zg = mm(_OFF_GR, RET_WIDTH)
    zgm = _dot_nt(wgmt_ref[...], h)
    cq = _rms(z_lat[:, :Q_LORA], qg_ref[...]).astype(BF16)
    qq = _dot(cq, wuq_ref[...])
    ckv = _rms(z_ckv, kvg_ref[...])
    ckv_b = ckv.astype(BF16)
    k_nope = _dot(ckv_b, wuk_ref[...])
    vt = _dot_nt(wuvt_ref[...], ckv_b)

    rc = tab_ref[:, _TAB_RC:_TAB_RC + LANES]
    rs = tab_ref[:, _TAB_RS:_TAB_RS + LANES]
    mc = tab_ref[:, _TAB_MC:_TAB_MC + LANES]
    ms_down = tab_ref[:, _TAB_MS_DOWN:_TAB_MS_DOWN + LANES]
    ms_up = tab_ref[:, _TAB_MS_UP:_TAB_MS_UP + LANES]

    def ret_rope(z):
        return z * rc + pltpu.roll(z, RET_DK // 2, 1) * rs

    def mla_rope(z):
        return (z * mc + pltpu.roll(z, LANES - MLA_ROPE // 2, 1) * ms_down
                + pltpu.roll(z, MLA_ROPE // 2, 1) * ms_up)

    for i in range(RET_HEADS):
        sl = slice(i * RET_DK, (i + 1) * RET_DK)
        qr_ref[0, :, sl] = ret_rope(zq[:, sl]).astype(qr_ref.dtype)
        kr_ref[0, :, sl] = ret_rope(zk[:, sl]) * (RET_DK ** -0.5)
    vr_ref[0] = zv.astype(vr_ref.dtype)
    sgr_ref[0] = _silu(zg)
    sgmt = _silu(zgm)
    ckv_ref[0] = ckv
    kpe = mla_rope(z_lat[:, Q_LORA:])
    kpe_ref[0] = kpe[:, MLA_NOPE:MLA_NOPE + MLA_ROPE]
    q_sq, k_sq = [], []
    for i in range(MLA_HEADS):
        sl = slice(i * MLA_HEAD_PAD, (i + 1) * MLA_HEAD_PAD)
        sgmt_ref[0, i] = sgmt[i * MLA_V:(i + 1) * MLA_V, :]
        q_head = mla_rope(qq[:, sl]) * MLA_Q_SCALE
        k_head = k_nope[:, sl] + kpe
        q_ref[0, i] = q_head.astype(BF16)
        k_ref[0, i] = k_head.astype(BF16)
        vt_ref[0, i] = vt[i * MLA_V:(i + 1) * MLA_V, :].astype(BF16)
        q_sq.append((q_head * q_head).astype(BF16))
        k_sq.append((k_head * k_head).astype(BF16))
    for sq, out_ref in ((q_sq, qn2_ref), (k_sq, kn2_ref)):
        norms2 = _dot(jnp.concatenate(sq, axis=1), hsel_ref[...])
        out_ref[0, 0] = jnp.broadcast_to(jnp.max(norms2, axis=0, keepdims=True), (8, LANES))


def _proj(x3, tab, w, tm, ret_dtype):
    nb, L, D = x3.shape
    tok = lambda n: pl.BlockSpec((1, tm, n), lambda j, b: (b, j, 0))
    head_tok = pl.BlockSpec((1, MLA_HEADS, tm, MLA_HEAD_PAD), lambda j, b: (b, 0, j, 0))
    head_feat = pl.BlockSpec((1, MLA_HEADS, MLA_V, tm), lambda j, b: (b, 0, 0, j))
    full = lambda a: pl.BlockSpec(a.shape, lambda j, b: (0,) * a.ndim)
    sds = jax.ShapeDtypeStruct
    tok_out = lambda n, dt=F32: (tok(n), sds((nb, L, n), dt))
    outs = [tok_out(RET_WIDTH, ret_dtype), tok_out(RET_WIDTH), tok_out(RET_WIDTH, ret_dtype), tok_out(RET_WIDTH),
            (head_feat, sds((nb, MLA_HEADS, MLA_V, L), F32)),
            (head_tok, sds((nb, MLA_HEADS, L, MLA_HEAD_PAD), BF16)),
            (head_tok, sds((nb, MLA_HEADS, L, MLA_HEAD_PAD), BF16)),
            (head_feat, sds((nb, MLA_HEADS, MLA_V, L), BF16)),
            tok_out(KV_LORA), tok_out(MLA_ROPE)]
    norm_spec = pl.BlockSpec((1, 1, 8, LANES), lambda j, b: (b, j, 0, 0))
    outs += [(norm_spec, sds((nb, L // tm, 8, LANES), F32))] * 2
    ws = [w["norm_g"], w["w_a"], w["w_gmt"], w["q_g"], w["kv_g"], w["w_uq"], w["w_uk"], w["w_uvt"], w["head_sel"]]
    return pl.pallas_call(
        _proj_kernel,
        grid=(L // tm, nb),
        in_specs=[tok(D), pl.BlockSpec((tm, _TAB_W), lambda j, b: (j, 0))] + [full(a) for a in ws],
        out_specs=[o[0] for o in outs],
        out_shape=[o[1] for o in outs],
        compiler_params=pltpu.CompilerParams(
            dimension_semantics=("arbitrary", "arbitrary"), vmem_limit_bytes=VMEM_LIMIT_BYTES),
        name="proj",
    )(x3, tab, *ws)


def _ret_kernel(q_ref, k_ref, v_ref, sg_ref, st_in_ref, dec_ref, qd_ref, kd_ref, sd_ref,
                o_ref, st_out_ref, st_scr, pad_scr, *, group, rows, chunk):
    c = pl.program_id(1)

    @pl.when(c == 0)
    def _():
        st_scr[...] = st_in_ref[...]
        pad_scr[...] = jnp.zeros_like(pad_scr)

    units = [(g, i) for g in range(group) for i in range(RET_HEADS)]

    def operand(ref, g, sl, slot):
        if rows == chunk:
            return ref[g, :, sl]
        idx = (g * RET_HEADS + sl.start // RET_DK) * 3 + slot
        pad_scr[idx, 0:rows, :] = ref[g, :, sl].astype(F32)
        return pad_scr[idx]

    def products(g, i):
        sl = slice(i * RET_DK, (i + 1) * RET_DK)
        q = operand(q_ref, g, sl, 0).astype(BF16)
        k = operand(k_ref, g, sl, 1)
        v = operand(v_ref, g, sl, 2).astype(BF16)
        state = st_scr[g, i]
        return q, k, v, state, _dot_nt(q, k.astype(BF16)), _dot(q, state.astype(BF16))

    ahead = [products(*u) for u in units[:RET_LOOKAHEAD]]
    for t, (g, i) in enumerate(units):
        sl = slice(i * RET_DK, (i + 1) * RET_DK)
        q, k, v, state, a, cross = ahead.pop(0)
        if t + RET_LOOKAHEAD < len(units):
            ahead.append(products(*units[t + RET_LOOKAHEAD]))
        o = _dot((a * dec_ref[i]).astype(BF16), v) + cross * qd_ref[i]
        o = o * lax.rsqrt(jnp.mean(o * o, axis=-1, keepdims=True) + EPS)
        o_ref[g, :, sl] = (o[0:rows] * sg_ref[g, :, sl]).astype(BF16)
        kd = (k * kd_ref[i]).astype(BF16)
        st_scr[g, i] = sd_ref[i] * state + _dot_tn(kd, v)

    @pl.when(c == pl.num_programs(1) - 1)
    def _():
        st_out_ref[...] = st_scr[...]


RET_LOOKAHEAD = 3


def _ret_tables(rows, chunk):
    lg = jnp.log1p(-jnp.exp2(-5.0 - jnp.arange(RET_HEADS, dtype=F32)))
    idx = jnp.arange(rows, dtype=F32)
    diff = idx[:, None] - idx[None, :]
    decay = jnp.where(diff[None] >= 0, jnp.exp(jnp.maximum(diff, 0.0)[None] * lg[:, None, None]), 0.0)
    q_decay = jnp.exp((idx[None, :] + 1.0) * lg[:, None])
    k_decay = jnp.exp((rows - 1.0 - idx)[None, :] * lg[:, None])
    s_decay = jnp.exp(rows * lg)
    pad = chunk - rows
    decay = jnp.pad(decay, ((0, 0), (0, pad), (0, pad)))
    bcast = lambda t: jnp.broadcast_to(jnp.pad(t, ((0, 0), (0, pad)))[:, :, None], (RET_HEADS, chunk, LANES))
    s_decay = jnp.broadcast_to(s_decay[:, None, None], (RET_HEADS, RET_DK, LANES))
    return decay, bcast(q_decay), bcast(k_decay), s_decay


def _retention(q, k, v, sg, state, group, rows):
    nb, L, _ = q.shape
    chunk = max(rows, RET_CHUNK)
    dec, qd, kd, sd = _ret_tables(rows, chunk)
    tok = pl.BlockSpec((group, rows, RET_WIDTH), lambda b, c: (b, c, 0))
    st = pl.BlockSpec((group, RET_HEADS, RET_DK, RET_DV), lambda b, c: (b, 0, 0, 0))
    full = lambda a: pl.BlockSpec(a.shape, lambda b, c: (0,) * a.ndim)
    return pl.pallas_call(
        functools.partial(_ret_kernel, group=group, rows=rows, chunk=chunk),
        grid=(nb // group, L // rows),
        in_specs=[tok, tok, tok, tok, st, full(dec), full(qd), full(kd), full(sd)],
        out_specs=[tok, st],
        out_shape=[jax.ShapeDtypeStruct((nb, L, RET_WIDTH), BF16),
                   jax.ShapeDtypeStruct(state.shape, F32)],
        scratch_shapes=[pltpu.VMEM((group, RET_HEADS, RET_DK, RET_DV), F32),
                        pltpu.VMEM((3 * group * RET_HEADS if rows < chunk else 3, chunk, RET_DK), F32)],
        compiler_params=pltpu.CompilerParams(
            dimension_semantics=("arbitrary", "arbitrary"), vmem_limit_bytes=VMEM_LIMIT_BYTES),
        name="retention",
    )(q, k, v, sg, state, dec, qd, kd, sd)


def _flash_kernel(qi_ref, ki_ref, q_ref, k_ref, vt_ref, sgt_ref, qn2_ref, kn2_ref, o_ref, m_scr, l_scr, acc_scr,
                  *, tile, sub_q, sub_k, lookahead):
    step = pl.program_id(1)
    qi = qi_ref[step]
    ki = ki_ref[step]

    @pl.when(ki == 0)
    def _():
        m_scr[...] = jnp.full_like(m_scr, -jnp.inf)
        l_scr[...] = jnp.zeros_like(l_scr)
        acc_scr[...] = jnp.zeros_like(acc_scr)

    bound2 = (jnp.max(qn2_ref[0], axis=0) * jnp.max(kn2_ref[0], axis=0))[0:1] * FLASH_BOUND_SLACK
    lane = lax.broadcasted_iota(jnp.int32, (1, LANES), 1)
    room = jnp.full((1, LANES), jnp.inf, F32)
    for i in range(MLA_HEADS):
        room = jnp.where(lane == i, jnp.min(m_scr[i], axis=1, keepdims=True) + FLASH_STALE_MARGIN, room)
    unsafe = jnp.where(room > 0.0, jnp.where(bound2 <= room * room, 0.0, 1.0), 1.0)
    keep_stabiliser = jnp.sum(jnp.where(lane < MLA_HEADS, unsafe, 0.0)) == 0.0

    def accumulate(diagonal, stale, heads_per_iter):
        units = [(hh, a, b) for hh in range(heads_per_iter)
                 for a in range(tile // sub_q) for b in range(tile // sub_k)
                 if not diagonal or b * sub_k <= (a + 1) * sub_q - 1]

        def head_group(g, carry):
            def scores(hh, a, b):
                i = g * heads_per_iter + hh
                return _dot_nt(k_ref[0, i, pl.ds(b * sub_k, sub_k), :], q_ref[0, i, pl.ds(a * sub_q, sub_q), :])

            ahead = [scores(*u) for u in units[:lookahead]]
            for t, (hh, a, b) in enumerate(units):
                i = g * heads_per_iter + hh
                s = ahead.pop(0)
                if t + lookahead < len(units):
                    ahead.append(scores(*units[t + lookahead]))
                qs = pl.ds(a * sub_q, sub_q)
                if t == 0 or units[t - 1][:2] != (hh, a):
                    m, l, acc = m_scr[i, :, qs], l_scr[i, :, qs], acc_scr[i, :, qs]
                if diagonal and (b + 1) * sub_k - 1 > a * sub_q:
                    key = b * sub_k + lax.broadcasted_iota(jnp.int32, (sub_k, sub_q), 0)
                    query = a * sub_q + lax.broadcasted_iota(jnp.int32, (sub_k, sub_q), 1)
                    s = jnp.where(key <= query, s, -jnp.inf)
                if not stale:
                    m_next = jnp.maximum(m, jnp.max(s, axis=0, keepdims=True))
                    alpha = jnp.exp2(m - m_next)
                    l, acc, m = alpha * l, alpha * acc, m_next
                p = jnp.exp2(s - m).astype(BF16)
                vt1 = jnp.concatenate([vt_ref[0, i, :, pl.ds(b * sub_k, sub_k)], jnp.ones((16, sub_k), BF16)], axis=0)
                pv = _dot(vt1, p)
                l = l + pv[MLA_V:MLA_V + 1]
                acc = acc + pv[:MLA_V]
                if t + 1 == len(units) or units[t + 1][:2] != (hh, a):
                    l_scr[i, :, qs], acc_scr[i, :, qs] = l, acc
                    if not stale:
                        m_scr[i, :, qs] = m
            return carry

        lax.fori_loop(0, MLA_HEADS // heads_per_iter, head_group, 0)

    for diagonal in (False, True):
        on_tile = (ki == qi) if diagonal else (ki < qi)

        @pl.when(jnp.logical_and(on_tile, keep_stabiliser))
        def _():
            accumulate(diagonal, True, FLASH_HEADS_PER_ITER)

        @pl.when(jnp.logical_and(on_tile, jnp.logical_not(keep_stabiliser)))
        def _():
            accumulate(diagonal, False, FLASH_HEADS_PER_ITER_EXACT)

    @pl.when(ki == qi)
    def _():
        for i in range(MLA_HEADS):
            o_ref[0, i] = (acc_scr[i] / l_scr[i] * sgt_ref[0, i]).astype(BF16)


FLASH_SUB_Q = 512
FLASH_SUB_K = 512
FLASH_LOOKAHEAD = 2
FLASH_HEADS_PER_ITER = 4
FLASH_HEADS_PER_ITER_EXACT = 2
FLASH_STALE_MARGIN = 60.0
FLASH_BOUND_SLACK = 1.05


def _flash(q, k, vt, sgt, qn2, kn2, tile, sub_q, sub_k, lookahead):
    nb, _, S, _ = q.shape
    nt = S // tile
    norms_per_tile = kn2.shape[1] // nt
    pairs = [(a, b) for a in range(nt) for b in range(a + 1)]
    qi = jnp.asarray(np.array([p[0] for p in pairs], np.int32))
    ki = jnp.asarray(np.array([p[1] for p in pairs], np.int32))
    q_tok = pl.BlockSpec((1, MLA_HEADS, tile, MLA_HEAD_PAD), lambda b, s, qi, ki: (b, 0, qi[s], 0))
    k_tok = pl.BlockSpec((1, MLA_HEADS, tile, MLA_HEAD_PAD), lambda b, s, qi, ki: (b, 0, ki[s], 0))
    k_feat = pl.BlockSpec((1, MLA_HEADS, MLA_V, tile), lambda b, s, qi, ki: (b, 0, 0, ki[s]))
    q_feat = pl.BlockSpec((1, MLA_HEADS, MLA_V, tile), lambda b, s, qi, ki: (b, 0, 0, qi[s]))
    q_norm = pl.BlockSpec((1, norms_per_tile, 8, LANES), lambda b, s, qi, ki: (b, qi[s], 0, 0))
    k_norm = pl.BlockSpec((1, norms_per_tile, 8, LANES), lambda b, s, qi, ki: (b, ki[s], 0, 0))
    return pl.pallas_call(
        functools.partial(_flash_kernel, tile=tile, sub_q=sub_q, sub_k=sub_k, lookahead=lookahead),
        grid_spec=pltpu.PrefetchScalarGridSpec(
            num_scalar_prefetch=2,
            grid=(nb, len(pairs)),
            in_specs=[q_tok, k_tok, k_feat, q_feat, q_norm, k_norm],
            out_specs=q_feat,
            scratch_shapes=[pltpu.VMEM((MLA_HEADS, 1, tile), F32),
                            pltpu.VMEM((MLA_HEADS, 1, tile), F32),
                            pltpu.VMEM((MLA_HEADS, MLA_V, tile), F32)]),
        out_shape=jax.ShapeDtypeStruct((nb, MLA_HEADS, MLA_V, S), BF16),
        compiler_params=pltpu.CompilerParams(
            dimension_semantics=("arbitrary", "arbitrary"), vmem_limit_bytes=VMEM_LIMIT_BYTES),
        name="flash",
    )(qi, ki, q, k, vt, sgt, qn2, kn2)


_DECODE_SLOTS = 3
DECODE_PAGES = 64
DECODE_LOOKAHEAD = 3


def _decode_kernel(pt_ref, q_ref, knew_ref, ckvnew_ref, sg_ref, wabs_ref, wuv_ref, lat_hbm, pet_hbm,
                   o_ref, lat_buf, pet_buf, sem, qlat_scr, m_scr, l_scr, acc_scr,
                   *, pages, n_steps, page, n_new, chunk):
    b = pl.program_id(0)
    j = pl.program_id(1)
    n_total = pl.num_programs(0) * n_steps
    g = b * n_steps + j
    rows = MLA_HEADS * n_new

    def copies(step, slot):
        out = []
        for p in range(pages):
            pid = pt_ref[step * pages + p]
            out.append(pltpu.make_async_copy(
                lat_hbm.at[pid], lat_buf.at[slot, pl.ds(p * page, page)], sem.at[slot]))
            out.append(pltpu.make_async_copy(
                pet_hbm.at[pid], pet_buf.at[slot, :, pl.ds(p * page, page)], sem.at[slot]))
        return out

    @pl.when(g == 0)
    def _():
        for cp in copies(0, 0) + copies(1, 1):
            cp.start()

    @pl.when(j == 0)
    def _():
        qf = q_ref[0].astype(F32)
        for i in range(MLA_HEADS):
            qlat_scr[i * n_new:(i + 1) * n_new, :] = _dot(qf[i * n_new:(i + 1) * n_new].astype(BF16), wabs_ref[i])
        m_scr[...] = jnp.full_like(m_scr, -jnp.inf)
        l_scr[...] = jnp.zeros_like(l_scr)
        acc_scr[...] = jnp.zeros_like(acc_scr)

    slot = lax.rem(g, _DECODE_SLOTS)
    for cp in copies(g, slot):
        cp.wait()
    gather = copies(jnp.minimum(g + 2, n_total - 1), lax.rem(g + 2, _DECODE_SLOTS))
    n_chunks = pages * page // chunk
    per_chunk = -(-len(gather) // n_chunks)

    qlat = qlat_scr[...].astype(BF16)
    q_lat = qlat[:, :KV_LORA]
    q_pe = qlat[:, KV_LORA:]

    def online(carry, s, values):
        m, l, acc = carry
        m_next = jnp.maximum(m, jnp.max(s, axis=1, keepdims=True))
        alpha = jnp.exp2(m - m_next)
        p = jnp.exp2(s - m_next[:, 0:1])
        l = alpha * l + jnp.sum(p, axis=1, keepdims=True)
        acc = jnp.concatenate([alpha, alpha], axis=1) * acc + _dot(p.astype(BF16), values)
        return m_next, l, acc

    def chunk_scores(c):
        lat = lat_buf[slot, pl.ds(c * chunk, chunk), :].astype(BF16)
        pet = pet_buf[slot, :, pl.ds(c * chunk, chunk)].astype(BF16)
        return lat, _dot_nt(q_lat, lat) + _dot(q_pe, pet)

    carry = (m_scr[...], l_scr[...], acc_scr[...])
    ahead = [chunk_scores(c) for c in range(min(DECODE_LOOKAHEAD, n_chunks))]
    for c in range(n_chunks):
        lat, s = ahead.pop(0)
        if c + DECODE_LOOKAHEAD < n_chunks:
            ahead.append(chunk_scores(c + DECODE_LOOKAHEAD))
        for cp in gather[c * per_chunk:(c + 1) * per_chunk]:
            cp.start()
        carry = online(carry, s, lat)
    m_scr[...], l_scr[...], acc_scr[...] = carry

    @pl.when(j == n_steps - 1)
    def _():
        ckv_new = ckvnew_ref[0].astype(BF16)
        k_new = knew_ref[0].astype(BF16)
        s = _dot_nt(q_lat, ckv_new) + _dot_nt(q_pe, k_new)
        t = lax.rem(lax.broadcasted_iota(jnp.int32, s.shape, 0), n_new)
        u = lax.broadcasted_iota(jnp.int32, s.shape, 1)
        m, l, acc = online(carry, jnp.where(u <= t, s, -jnp.inf), ckv_new)
        o_lat = (acc / jnp.concatenate([l, l], axis=1)).astype(BF16)
        y = _dot(o_lat, wuv_ref[...])
        row_head = lax.broadcasted_iota(jnp.int32, y.shape, 0) // n_new
        col_head = lax.broadcasted_iota(jnp.int32, y.shape, 1) // MLA_V
        y = jnp.where(row_head == col_head, y, 0.0)
        out = y[0:n_new]
        for i in range(1, MLA_HEADS):
            out = out + y[i * n_new:(i + 1) * n_new]
        o_ref[0] = (out * sg_ref[0]).astype(BF16)

    @pl.when(g == n_total - 1)
    def _():
        for ahead in (1, 2):
            for cp in copies(g, lax.rem(g + ahead, _DECODE_SLOTS)):
                cp.wait()


def _decode(page_table, q, k_new, ckv_new, sg, w, cache_lat, cache_pet, pages, chunk):
    nb, rows, _ = q.shape
    n_new = rows // MLA_HEADS
    n_pool, page, _ = cache_lat.shape
    n_pages = page_table.shape[1]
    n_steps = n_pages // pages
    assert nb * n_steps >= _DECODE_SLOTS
    tk = pages * page
    per_b = lambda n0, n1: pl.BlockSpec((1, n0, n1), lambda b, j, pt: (b, 0, 0))
    full = lambda a: pl.BlockSpec(a.shape, lambda b, j, pt: (0,) * a.ndim)
    any_spec = pl.BlockSpec(memory_space=pl.ANY)
    return pl.pallas_call(
        functools.partial(_decode_kernel, pages=pages, n_steps=n_steps, page=page, n_new=n_new, chunk=chunk),
        grid_spec=pltpu.PrefetchScalarGridSpec(
            num_scalar_prefetch=1,
            grid=(nb, n_steps),
            in_specs=[per_b(rows, MLA_HEAD_PAD), per_b(LANES, MLA_ROPE), per_b(LANES, KV_LORA),
                      per_b(n_new, MLA_WIDTH), full(w["w_abs"]), full(w["w_uv"]), any_spec, any_spec],
            out_specs=per_b(n_new, MLA_WIDTH),
            scratch_shapes=[pltpu.VMEM((_DECODE_SLOTS, tk, KV_LORA), F32),
                            pltpu.VMEM((_DECODE_SLOTS, MLA_ROPE, tk), F32),
                            pltpu.SemaphoreType.DMA((_DECODE_SLOTS,)),
                            pltpu.VMEM((rows, KV_LORA + MLA_ROPE), F32),
                            pltpu.VMEM((rows, LANES), F32),
                            pltpu.VMEM((rows, LANES), F32),
                            pltpu.VMEM((rows, KV_LORA), F32)]),
        out_shape=jax.ShapeDtypeStruct((nb, n_new, MLA_WIDTH), BF16),
        compiler_params=pltpu.CompilerParams(
            dimension_semantics=("arbitrary", "arbitrary"), vmem_limit_bytes=VMEM_LIMIT_BYTES),
        name="decode",
    )(page_table.reshape(-1), q, k_new, ckv_new, sg, w["w_abs"], w["w_uv"], cache_lat, cache_pet)


def _merge_kernel(x_ref, ret_ref, mla_ref, ng_ref, wg_ref, wbr_ref, wbm_ref, wo_ref, fg_ref, y_ref,
                  *, mla_feature_major):
    x = x_ref[0]
    D = x.shape[-1]
    h = _rms(x, ng_ref[...]).astype(BF16)
    gates = jax.nn.sigmoid(_dot(h, wg_ref[...]))
    p_mla = _dot_tn(mla_ref[0], wbm_ref[...]) if mla_feature_major else _dot(mla_ref[0], wbm_ref[...])
    merged = gates[:, :D] * _dot(ret_ref[0], wbr_ref[...]) + gates[:, D:] * p_mla
    y = x + _dot(merged.astype(BF16), wo_ref[...])
    y_ref[0] = _rms(y, fg_ref[...])


def _merge(x3, ret, mla, w, tm, mla_feature_major):
    nb, L, D = x3.shape
    tok = lambda n: pl.BlockSpec((1, tm, n), lambda b, j: (b, j, 0))
    feat = pl.BlockSpec((1, MLA_WIDTH, tm), lambda b, j: (b, 0, j))
    full = lambda a: pl.BlockSpec(a.shape, lambda b, j: (0,) * a.ndim)
    ws = [w["norm_g"], w["w_gate"], w["w_br_ret"], w["w_br_mla"], w["w_out"], w["final_g"]]
    return pl.pallas_call(
        functools.partial(_merge_kernel, mla_feature_major=mla_feature_major),
        grid=(nb, L // tm),
        in_specs=[tok(D), tok(RET_WIDTH), feat if mla_feature_major else tok(MLA_WIDTH)] + [full(a) for a in ws],
        out_specs=tok(D),
        out_shape=jax.ShapeDtypeStruct((nb, L, D), F32),
        compiler_params=pltpu.CompilerParams(
            dimension_semantics=("arbitrary", "arbitrary"), vmem_limit_bytes=VMEM_LIMIT_BYTES),
        name="merge",
    )(x3, ret, mla, *ws)


def _prep_weights(norm_gain, w_in, q_norm_gain, kv_norm_gain, w_uq, w_uk, w_uv,
                  w_branch_ret, w_branch_mla, w_out, final_norm_gain):
    D = w_in.shape[0]
    sizes = (RET_WIDTH, RET_WIDTH, RET_WIDTH, RET_WIDTH, Q_LORA, KV_LORA, MLA_ROPE, MLA_WIDTH, D, D)
    pts = np.cumsum(sizes)[:-1].tolist()
    wq_r, wk_r, wv_r, wg_r, wc_q, wc_kv, wk_pe, wg_m, wmg_r, wmg_m = jnp.split(w_in, pts, axis=1)
    place = lambda t: jnp.pad(t, ((0, 0), (MLA_NOPE, MLA_HEAD_PAD - MLA_NOPE - MLA_ROPE)))
    w_a = jnp.concatenate([wq_r, wk_r, wv_r, wg_r, wc_q, place(wk_pe), wc_kv], axis=1)
    assert w_a.shape[1] == _N_PROJ

    uq = w_uq.reshape(Q_LORA, MLA_HEADS, MLA_NOPE + MLA_ROPE)
    tail = MLA_HEAD_PAD - MLA_NOPE - MLA_ROPE
    w_uq2 = jnp.pad(uq, ((0, 0), (0, 0), (0, tail))).reshape(Q_LORA, -1)
    uk_pad = jnp.pad(w_uk, ((0, 0), (0, 0), (0, MLA_HEAD_PAD - MLA_NOPE))).reshape(KV_LORA, -1)
    uv = w_uv.reshape(KV_LORA, MLA_WIDTH)

    w_abs = jnp.zeros((MLA_HEADS, MLA_HEAD_PAD, KV_LORA + MLA_ROPE), F32)
    w_abs = w_abs.at[:, :MLA_NOPE, :KV_LORA].set(jnp.transpose(w_uk, (1, 2, 0)))
    w_abs = w_abs.at[:, MLA_NOPE:MLA_NOPE + MLA_ROPE, KV_LORA:].set(jnp.eye(MLA_ROPE, dtype=F32)[None])

    head_sel = jnp.repeat(jnp.eye(MLA_HEADS, LANES, dtype=F32), MLA_HEAD_PAD, axis=0)

    return dict(
        head_sel=head_sel.astype(BF16),
        norm_g=norm_gain.reshape(1, D), q_g=q_norm_gain.reshape(1, Q_LORA), kv_g=kv_norm_gain.reshape(1, KV_LORA),
        final_g=final_norm_gain.reshape(1, D),
        w_a=w_a.astype(BF16), w_gmt=wg_m.T.astype(BF16), w_uq=w_uq2.astype(BF16), w_uk=uk_pad.astype(BF16),
        w_uvt=uv.T.astype(BF16), w_uv=uv.astype(BF16), w_abs=w_abs.astype(BF16),
        w_gate=jnp.concatenate([wmg_r, wmg_m], axis=1).astype(BF16),
        w_br_ret=w_branch_ret.astype(BF16), w_br_mla=w_branch_mla.astype(BF16), w_out=w_out.astype(BF16))


def _rope_table(pos):
    posf = pos.astype(F32)[:, None]

    def cos_sin(d):
        inv = jnp.power(ROPE_BASE, -jnp.arange(d // 2, dtype=F32) * (2.0 / d))
        ang = posf * inv[None, :]
        return jnp.cos(ang), jnp.sin(ang)

    rc, rs = cos_sin(RET_DK)
    mc, ms = cos_sin(MLA_ROPE)
    n = pos.shape[0]
    ones = jnp.ones((n, MLA_NOPE), F32)
    zeros = lambda width: jnp.zeros((n, width), F32)
    half = MLA_ROPE // 2
    tail = MLA_HEAD_PAD - MLA_NOPE - MLA_ROPE
    m_cos = jnp.concatenate([ones, mc, mc, zeros(tail)], axis=1)
    m_sin_down = jnp.concatenate([zeros(MLA_NOPE), -ms, zeros(half + tail)], axis=1)
    m_sin_up = jnp.concatenate([zeros(MLA_NOPE + half), ms, zeros(tail)], axis=1)
    return jnp.concatenate([rc, rc, -rs, rs, m_cos, m_sin_down, m_sin_up], axis=1)


def kernel(x_prompt, x_sample, cache_kv_latent, cache_k_rope, state_retention, page_table, norm_gain, w_in,
           q_norm_gain, kv_norm_gain, w_uq, w_uk, w_uv, w_branch_ret, w_branch_mla, w_out, final_norm_gain):
    assert norm_gain.shape[0] == 1, "single-layer kernel"
    B, S, D = x_prompt.shape
    NB, T, _ = x_sample.shape
    page = cache_kv_latent.shape[2]
    n_pages = page_table.shape[1]
    past_len = n_pages * page
    w = _prep_weights(norm_gain[0], w_in[0], q_norm_gain[0], kv_norm_gain[0], w_uq[0], w_uk[0], w_uv[0],
                      w_branch_ret[0], w_branch_mla[0], w_out[0], final_norm_gain)

    tm = min(256, S)
    qr, kr, vr, sgr, sgmt, q, k, vt, ckv, kpe, qn2, kn2 = _proj(x_prompt, _rope_table(jnp.arange(S, dtype=jnp.int32)), w, tm, BF16)
    rows = min(RET_CHUNK, S)
    ret_o, ret_state = _retention(qr, kr, vr, sgr, jnp.zeros((B, RET_HEADS, RET_DK, RET_DV), F32), B, rows)
    tile = min(1024, S)
    mla_t = _flash(q, k, vt, sgmt, qn2, kn2, tile, min(FLASH_SUB_Q, tile), min(FLASH_SUB_K, tile), FLASH_LOOKAHEAD)
    y_prompt = _merge(x_prompt, ret_o, mla_t.reshape(B, MLA_WIDTH, S), w, min(512, S), True)

    n_tok = NB * T
    tms = min(256, n_tok)
    nbs = n_tok // tms
    pos_s = past_len + jnp.arange(T, dtype=jnp.int32)
    tab_s = jnp.tile(_rope_table(pos_s), (tms // T, 1))
    qr, kr, vr, sgr, sgmt, q, _, _, ckv_s, kpe_s, _, _ = _proj(x_sample.reshape(nbs, tms, D), tab_s, w, tms, F32)
    per_seq = lambda t: t.reshape(NB, T, t.shape[-1])
    qr, kr, vr, sgr, ckv_s, kpe_s = map(per_seq, (qr, kr, vr, sgr, ckv_s, kpe_s))
    ret_o_s, ret_state_s = _retention(qr, kr, vr, sgr, state_retention[0], 8, T)
    q_s = q.reshape(nbs, MLA_HEADS, tms // T, T, MLA_HEAD_PAD).transpose(0, 2, 1, 3, 4)
    q_s = q_s.reshape(NB, MLA_HEADS * T, MLA_HEAD_PAD)
    sgm_s = sgmt.reshape(nbs, MLA_WIDTH, tms).transpose(0, 2, 1).reshape(NB, T, MLA_WIDTH)
    pad_new = lambda t: jnp.pad(t, ((0, 0), (0, LANES - T), (0, 0)))
    decode_pages = math.gcd(n_pages, DECODE_PAGES)
    mla_o_s = _decode(page_table, q_s, pad_new(kpe_s), pad_new(ckv_s), sgm_s, w,
                      cache_kv_latent[0], jnp.swapaxes(cache_k_rope[0], 1, 2),
                      pages=decode_pages, chunk=min(1024, decode_pages * page))
    y_sample = _merge(x_sample.reshape(1, n_tok, D), ret_o_s.reshape(1, n_tok, RET_WIDTH),
                      mla_o_s.reshape(1, n_tok, MLA_WIDTH), w, min(512, n_tok), False).reshape(NB, T, D)

    return (y_prompt, y_sample, ckv[None], kpe[None], ret_state[None],
            ckv_s[None], kpe_s[None], ret_state_s[None])
```

```python
import functools
import math

import jax
import jax.numpy as jnp
import numpy as np
from jax import lax
from jax.experimental import pallas as pl
from jax.experimental.pallas import tpu as pltpu

F32 = jnp.float32
BF16 = jnp.bfloat16

RET_HEADS = 4
RET_DK = 128
RET_DV = 128
RET_WIDTH = RET_HEADS * RET_DV
RET_CHUNK = 128
MLA_HEADS = 8
MLA_NOPE = 64
MLA_ROPE = 32
MLA_V = 64
MLA_WIDTH = MLA_HEADS * MLA_V
MLA_HEAD_PAD = 128
Q_LORA = 384
KV_LORA = 256
ROPE_BASE = 10000.0
EPS = 1e-6
MLA_Q_SCALE = (MLA_NOPE + MLA_ROPE) ** -0.5 * math.log2(math.e)

LANES = 128
VMEM_LIMIT_BYTES = 56 * 1024 * 1024

_OFF_QR = 0
_OFF_KR = _OFF_QR + RET_WIDTH
_OFF_VR = _OFF_KR + RET_WIDTH
_OFF_GR = _OFF_VR + RET_WIDTH
_OFF_CQ = _OFF_GR + RET_WIDTH
_OFF_KPE = _OFF_CQ + Q_LORA
_OFF_CKV = _OFF_KPE + LANES
_N_PROJ = _OFF_CKV + KV_LORA

_TAB_RC, _TAB_RS, _TAB_MC, _TAB_MS_DOWN, _TAB_MS_UP = (i * LANES for i in range(5))
_TAB_W = 5 * LANES


def _rms(x, g):
    return x * lax.rsqrt(jnp.mean(x * x, axis=-1, keepdims=True) + EPS) * g


def _silu(x):
    return x * jax.nn.sigmoid(x)


def _dot(a, b):
    return jnp.dot(a, b, preferred_element_type=F32)


def _dot_nt(a, b):
    return lax.dot_general(a, b, (((1,), (1,)), ((), ())), preferred_element_type=F32)


def _dot_tn(a, b):
    return lax.dot_general(a, b, (((0,), (0,)), ((), ())), preferred_element_type=F32)


def _proj_kernel(x_ref, tab_ref, ng_ref, wa_ref, wgmt_ref, qg_ref, kvg_ref, wuq_ref, wuk_ref, wuvt_ref, hsel_ref,
                 qr_ref, kr_ref, vr_ref, sgr_ref, sgmt_ref, q_ref, k_ref, vt_ref, ckv_ref, kpe_ref, qn2_ref, kn2_ref):
    h = _rms(x_ref[0], ng_ref[...]).astype(BF16)

    def mm(lo, n):
        return _dot(h, wa_ref[:, lo:lo + n])

    z_lat = mm(_OFF_CQ, Q_LORA + LANES)
    z_ckv = mm(_OFF_CKV, KV_LORA)
    zq = mm(_OFF_QR, RET_WIDTH)
    zk = mm(_OFF_KR, RET_WIDTH)
    zv = mm(_OFF_VR, RET_WIDTH)
    zg = mm(_OFF_GR, RET_WIDTH)
    zgm = _dot_nt(wgmt_ref[...], h)
    cq = _rms(z_lat[:, :Q_LORA], qg_ref[...]).astype(BF16)
    qq = _dot(cq, wuq_ref[...])
    ckv = _rms(z_ckv, kvg_ref[...])
    ckv_b = ckv.astype(BF16)
    k_nope = _dot(ckv_b, wuk_ref[...])
    vt = _dot_nt(wuvt_ref[...], ckv_b)

    rc = tab_ref[:, _TAB_RC:_TAB_RC + LANES]
    rs = tab_ref[:, _TAB_RS:_TAB_RS + LANES]
    mc = tab_ref[:, _TAB_MC:_TAB_MC + LANES]
    ms_down = tab_ref[:, _TAB_MS_DOWN:_TAB_MS_DOWN + LANES]
    ms_up = tab_ref[:, _TAB_MS_UP:_TAB_MS_UP + LANES]

    def ret_rope(z):
        return z * rc + pltpu.roll(z, RET_DK // 2, 1) * rs

    def mla_rope(z):
        return (z * mc + pltpu.roll(z, LANES - MLA_ROPE // 2, 1) * ms_down
                + pltpu.roll(z, MLA_ROPE // 2, 1) * ms_up)

    for i in range(RET_HEADS):
        sl = slice(i * RET_DK, (i + 1) * RET_DK)
        qr_ref[0, :, sl] = ret_rope(zq[:, sl]).astype(qr_ref.dtype)
        kr_ref[0, :, sl] = ret_rope(zk[:, sl]) * (RET_DK ** -0.5)
    vr_ref[0] = zv.astype(vr_ref.dtype)
    sgr_ref[0] = _silu(zg)
    sgmt = _silu(zgm)
    ckv_ref[0] = ckv
    kpe = mla_rope(z_lat[:, Q_LORA:])
    kpe_ref[0] = kpe[:, MLA_NOPE:MLA_NOPE + MLA_ROPE]
    q_sq, k_sq = [], []
    for i in range(MLA_HEADS):
        sl = slice(i * MLA_HEAD_PAD, (i + 1) * MLA_HEAD_PAD)
        sgmt_ref[0, i] = sgmt[i * MLA_V:(i + 1) * MLA_V, :]
        q_head = mla_rope(qq[:, sl]) * MLA_Q_SCALE
        k_head = k_nope[:, sl] + kpe
        q_ref[0, i] = q_head.astype(BF16)
        k_ref[0, i] = k_head.astype(BF16)
        vt_ref[0, i] = vt[i * MLA_V:(i + 1) * MLA_V, :].astype(BF16)
        q_sq.append((q_head * q_head).astype(BF16))
        k_sq.append((k_head * k_head).astype(BF16))
    for sq, out_ref in ((q_sq, qn2_ref), (k_sq, kn2_ref)):
        norms2 = _dot(jnp.concatenate(sq, axis=1), hsel_ref[...])
        out_ref[0, 0] = jnp.broadcast_to(jnp.max(norms2, axis=0, keepdims=True), (8, LANES))


def _proj(x3, tab, w, tm, ret_dtype):
    nb, L, D = x3.shape
    tok = lambda n: pl.BlockSpec((1, tm, n), lambda j, b: (b, j, 0))
    head_tok = pl.BlockSpec((1, MLA_HEADS, tm, MLA_HEAD_PAD), lambda j, b: (b, 0, j, 0))
    head_feat = pl.BlockSpec((1, MLA_HEADS, MLA_V, tm), lambda j, b: (b, 0, 0, j))
    full = lambda a: pl.BlockSpec(a.shape, lambda j, b: (0,) * a.ndim)
    sds = jax.ShapeDtypeStruct
    tok_out = lambda n, dt=F32: (tok(n), sds((nb, L, n), dt))
    outs = [tok_out(RET_WIDTH, ret_dtype), tok_out(RET_WIDTH), tok_out(RET_WIDTH, ret_dtype), tok_out(RET_WIDTH),
            (head_feat, sds((nb, MLA_HEADS, MLA_V, L), F32)),
            (head_tok, sds((nb, MLA_HEADS, L, MLA_HEAD_PAD), BF16)),
            (head_tok, sds((nb, MLA_HEADS, L, MLA_HEAD_PAD), BF16)),
            (head_feat, sds((nb, MLA_HEADS, MLA_V, L), BF16)),
            tok_out(KV_LORA), tok_out(MLA_ROPE)]
    norm_spec = pl.BlockSpec((1, 1, 8, LANES), lambda j, b: (b, j, 0, 0))
    outs += [(norm_spec, sds((nb, L // tm, 8, LANES), F32))] * 2
    ws = [w["norm_g"], w["w_a"], w["w_gmt"], w["q_g"], w["kv_g"], w["w_uq"], w["w_uk"], w["w_uvt"], w["head_sel"]]
    return pl.pallas_call(
        _proj_kernel,
        grid=(L // tm, nb),
        in_specs=[tok(D), pl.BlockSpec((tm, _TAB_W), lambda j, b: (j, 0))] + [full(a) for a in ws],
        out_specs=[o[0] for o in outs],
        out_shape=[o[1] for o in outs],
        compiler_params=pltpu.CompilerParams(
            dimension_semantics=("arbitrary", "arbitrary"), vmem_limit_bytes=VMEM_LIMIT_BYTES),
        name="proj",
    )(x3, tab, *ws)


def _ret_kernel(q_ref, k_ref, v_ref, sg_ref, st_in_ref, dec_ref, qd_ref, kd_ref, sd_ref,
                o_ref, st_out_ref, st_scr, pad_scr, *, group, rows, chunk):
    c = pl.program_id(1)

    @pl.when(c == 0)
    def _():
        st_scr[...] = st_in_ref[...]
        pad_scr[...] = jnp.zeros_like(pad_scr)

    units = [(g, i) for g in range(group) for i in range(RET_HEADS)]

    def operand(ref, g, sl, slot):
        if rows == chunk:
            return ref[g, :, sl]
        idx = (g * RET_HEADS + sl.start // RET_DK) * 3 + slot
        pad_scr[idx, 0:rows, :] = ref[g, :, sl].astype(F32)
        return pad_scr[idx]

    def products(g, i):
        sl = slice(i * RET_DK, (i + 1) * RET_DK)
        q = operand(q_ref, g, sl, 0).astype(BF16)
        k = operand(k_ref, g, sl, 1)
        v = operand(v_ref, g, sl, 2).astype(BF16)
        state = st_scr[g, i]
        return q, k, v, state, _dot_nt(q, k.astype(BF16)), _dot(q, state.astype(BF16))

    ahead = [products(*u) for u in units[:RET_LOOKAHEAD]]
    for t, (g, i) in enumerate(units):
        sl = slice(i * RET_DK, (i + 1) * RET_DK)
        q, k, v, state, a, cross = ahead.pop(0)
        if t + RET_LOOKAHEAD < len(units):
            ahead.append(products(*units[t + RET_LOOKAHEAD]))
        o = _dot((a * dec_ref[i]).astype(BF16), v) + cross * qd_ref[i]
        o = o * lax.rsqrt(jnp.mean(o * o, axis=-1, keepdims=True) + EPS)
        o_ref[g, :, sl] = (o[0:rows] * sg_ref[g, :, sl]).astype(BF16)
        kd = (k * kd_ref[i]).astype(BF16)
        st_scr[g, i] = sd_ref[i] * state + _dot_tn(kd, v)

    @pl.when(c == pl.num_programs(1) - 1)
    def _():
        st_out_ref[...] = st_scr[...]


RET_LOOKAHEAD = 3


def _ret_tables(rows, chunk):
    lg = jnp.log1p(-jnp.exp2(-5.0 - jnp.arange(RET_HEADS, dtype=F32)))
    idx = jnp.arange(rows, dtype=F32)
    diff = idx[:, None] - idx[None, :]
    decay = jnp.where(diff[None] >= 0, jnp.exp(jnp.maximum(diff, 0.0)[None] * lg[:, None, None]), 0.0)
    q_decay = jnp.exp((idx[None, :] + 1.0) * lg[:, None])
    k_decay = jnp.exp((rows - 1.0 - idx)[None, :] * lg[:, None])
    s_decay = jnp.exp(rows * lg)
    pad = chunk - rows
    decay = jnp.pad(decay, ((0, 0), (0, pad), (0, pad)))
    bcast = lambda t: jnp.broadcast_to(jnp.pad(t, ((0, 0), (0, pad)))[:, :, None], (RET_HEADS, chunk, LANES))
    s_decay = jnp.broadcast_to(s_decay[:, None, None], (RET_HEADS, RET_DK, LANES))
    return decay, bcast(q_decay), bcast(k_decay), s_decay


def _retention(q, k, v, sg, state, group, rows):
    nb, L, _ = q.shape
    chunk = max(rows, RET_CHUNK)
    dec, qd, kd, sd = _ret_tables(rows, chunk)
    tok = pl.BlockSpec((group, rows, RET_WIDTH), lambda b, c: (b, c, 0))
    st = pl.BlockSpec((group, RET_HEADS, RET_DK, RET_DV), lambda b, c: (b, 0, 0, 0))
    full = lambda a: pl.BlockSpec(a.shape, lambda b, c: (0,) * a.ndim)
    return pl.pallas_call(
        functools.partial(_ret_kernel, group=group, rows=rows, chunk=chunk),
        grid=(nb // group, L // rows),
        in_specs=[tok, tok, tok, tok, st, full(dec), full(qd), full(kd), full(sd)],
        out_specs=[tok, st],
        out_shape=[jax.ShapeDtypeStruct((nb, L, RET_WIDTH), BF16),
                   jax.ShapeDtypeStruct(state.shape, F32)],
        scratch_shapes=[pltpu.VMEM((group, RET_HEADS, RET_DK, RET_DV), F32),
                        pltpu.VMEM((3 * group * RET_HEADS if rows < chunk else 3, chunk, RET_DK), F32)],
        compiler_params=pltpu.CompilerParams(
            dimension_semantics=("arbitrary", "arbitrary"), vmem_limit_bytes=VMEM_LIMIT_BYTES),
        name="retention",
    )(q, k, v, sg, state, dec, qd, kd, sd)


def _flash_kernel(qi_ref, ki_ref, q_ref, k_ref, vt_ref, sgt_ref, qn2_ref, kn2_ref, o_ref,
                  m_scr, l_scr, acc_scr, mmin_scr, *, tile, sub_q, sub_k, lookahead):
    step = pl.program_id(1)
    qi = qi_ref[step]
    ki = ki_ref[step]

    bound2 = (jnp.max(qn2_ref[0], axis=0) * jnp.max(kn2_ref[0], axis=0))[0:1] * FLASH_BOUND_SLACK
    lane = lax.broadcasted_iota(jnp.int32, (1, LANES), 1)

    def no_head_exceeds(limit):
        bad = jnp.where(limit > 0.0, jnp.where(bound2 <= limit * limit, 0.0, 1.0), 1.0)
        return jnp.sum(jnp.where(lane < MLA_HEADS, bad, 0.0)) == 0.0

    @pl.when(ki == 0)
    def _():
        start = jnp.where(no_head_exceeds(jnp.full((1, LANES), FLASH_STALE_MARGIN, F32)), 0.0, -jnp.inf)
        m_scr[...] = jnp.full(m_scr.shape, start, F32)
        mmin_scr[...] = jnp.full(mmin_scr.shape, start, F32)
        l_scr[...] = jnp.zeros_like(l_scr)
        acc_scr[...] = jnp.zeros_like(acc_scr)

    keep_stabiliser = no_head_exceeds(mmin_scr[...] + FLASH_STALE_MARGIN)

    def accumulate(diagonal, stale, heads_per_iter):
        units = [(hh, a, b) for hh in range(heads_per_iter)
                 for a in range(tile // sub_q) for b in range(tile // sub_k)
                 if not diagonal or b * sub_k <= (a + 1) * sub_q - 1]

        def head_group(g, carry):
            def scores(hh, a, b):
                i = g * heads_per_iter + hh
                return _dot_nt(k_ref[0, i, pl.ds(b * sub_k, sub_k), :], q_ref[0, i, pl.ds(a * sub_q, sub_q), :])

            ahead = [scores(*u) for u in units[:lookahead]]
            for t, (hh, a, b) in enumerate(units):
                i = g * heads_per_iter + hh
                s = ahead.pop(0)
                if t + lookahead < len(units):
                    ahead.append(scores(*units[t + lookahead]))
                qs = pl.ds(a * sub_q, sub_q)
                if t == 0 or units[t - 1][:2] != (hh, a):
                    m, l, acc = m_scr[i, :, qs], l_scr[i, :, qs], acc_scr[i, :, qs]
                if diagonal and (b + 1) * sub_k - 1 > a * sub_q:
                    key = b * sub_k + lax.broadcasted_iota(jnp.int32, (sub_k, sub_q), 0)
                    query = a * sub_q + lax.broadcasted_iota(jnp.int32, (sub_k, sub_q), 1)
                    s = jnp.where(key <= query, s, -jnp.inf)
                if not stale:
                    m_next = jnp.maximum(m, jnp.max(s, axis=0, keepdims=True))
                    alpha = jnp.exp2(m - m_next)
                    l, acc, m = alpha * l, alpha * acc, m_next
                p = jnp.exp2(s - m).astype(BF16)
                vt1 = jnp.concatenate([vt_ref[0, i, :, pl.ds(b * sub_k, sub_k)], jnp.ones((16, sub_k), BF16)], axis=0)
                pv = _dot(vt1, p)
                l = l + pv[MLA_V:MLA_V + 1]
                acc = acc + pv[:MLA_V]
                if t + 1 == len(units) or units[t + 1][:2] != (hh, a):
                    l_scr[i, :, qs], acc_scr[i, :, qs] = l, acc
                    if not stale:
                        m_scr[i, :, qs] = m
            return carry

        lax.fori_loop(0, MLA_HEADS // heads_per_iter, head_group, 0)

    for diagonal in (False, True):
        on_tile = (ki == qi) if diagonal else (ki < qi)

        @pl.when(jnp.logical_and(on_tile, keep_stabiliser))
        def _():
            accumulate(diagonal, True, FLASH_HEADS_PER_ITER)

        @pl.when(jnp.logical_and(on_tile, jnp.logical_not(keep_stabiliser)))
        def _():
            accumulate(diagonal, False, FLASH_HEADS_PER_ITER_EXACT)
            smallest = jnp.full((1, LANES), jnp.inf, F32)
            for i in range(MLA_HEADS):
                smallest = jnp.where(lane == i, jnp.min(m_scr[i], axis=1, keepdims=True), smallest)
            mmin_scr[...] = smallest

    @pl.when(ki == qi)
    def _():
        for i in range(MLA_HEADS):
            o_ref[0, i] = (acc_scr[i] / l_scr[i] * sgt_ref[0, i]).astype(BF16)


FLASH_SUB_Q = 512
FLASH_SUB_K = 512
FLASH_LOOKAHEAD = 2
FLASH_HEADS_PER_ITER = 4
FLASH_HEADS_PER_ITER_EXACT = 2
FLASH_STALE_MARGIN = 60.0
FLASH_BOUND_SLACK = 1.05


def _flash(q, k, vt, sgt, qn2, kn2, tile, sub_q, sub_k, lookahead):
    nb, _, S, _ = q.shape
    nt = S // tile
    norms_per_tile = kn2.shape[1] // nt
    pairs = [(a, b) for a in range(nt) for b in range(a + 1)]
    qi = jnp.asarray(np.array([p[0] for p in pairs], np.int32))
    ki = jnp.asarray(np.array([p[1] for p in pairs], np.int32))
    q_tok = pl.BlockSpec((1, MLA_HEADS, tile, MLA_HEAD_PAD), lambda b, s, qi, ki: (b, 0, qi[s], 0))
    k_tok = pl.BlockSpec((1, MLA_HEADS, tile, MLA_HEAD_PAD), lambda b, s, qi, ki: (b, 0, ki[s], 0))
    k_feat = pl.BlockSpec((1, MLA_HEADS, MLA_V, tile), lambda b, s, qi, ki: (b, 0, 0, ki[s]))
    q_feat = pl.BlockSpec((1, MLA_HEADS, MLA_V, tile), lambda b, s, qi, ki: (b, 0, 0, qi[s]))
    q_norm = pl.BlockSpec((1, norms_per_tile, 8, LANES), lambda b, s, qi, ki: (b, qi[s], 0, 0))
    k_norm = pl.BlockSpec((1, norms_per_tile, 8, LANES), lambda b, s, qi, ki: (b, ki[s], 0, 0))
    return pl.pallas_call(
        functools.partial(_flash_kernel, tile=tile, sub_q=sub_q, sub_k=sub_k, lookahead=lookahead),
        grid_spec=pltpu.PrefetchScalarGridSpec(
            num_scalar_prefetch=2,
            grid=(nb, len(pairs)),
            in_specs=[q_tok, k_tok, k_feat, q_feat, q_norm, k_norm],
            out_specs=q_feat,
            scratch_shapes=[pltpu.VMEM((MLA_HEADS, 1, tile), F32),
                            pltpu.VMEM((MLA_HEADS, 1, tile), F32),
                            pltpu.VMEM((MLA_HEADS, MLA_V, tile), F32),
                            pltpu.VMEM((1, LANES), F32)]),
        out_shape=jax.ShapeDtypeStruct((nb, MLA_HEADS, MLA_V, S), BF16),
        compiler_params=pltpu.CompilerParams(
            dimension_semantics=("arbitrary", "arbitrary"), vmem_limit_bytes=VMEM_LIMIT_BYTES),
        name="flash",
    )(qi, ki, q, k, vt, sgt, qn2, kn2)


_DECODE_SLOTS = 3
DECODE_PAGES = 64
DECODE_CHUNK = 2048
DECODE_LOOKAHEAD = 3


def _decode_kernel(pt_ref, q_ref, knew_ref, ckvnew_ref, lat_hbm, pet_hbm,
                   o_ref, lat_buf, pet_buf, sem, m_scr, l_scr, acc_scr,
                   *, pages, n_steps, page, n_new, chunk):
    b = pl.program_id(0)
    j = pl.program_id(1)
    n_total = pl.num_programs(0) * n_steps
    g = b * n_steps + j

    def copies(step, slot):
        out = []
        for p in range(pages):
            pid = pt_ref[step * pages + p]
            out.append(pltpu.make_async_copy(
                lat_hbm.at[pid], lat_buf.at[slot, pl.ds(p * page, page)], sem.at[slot]))
            out.append(pltpu.make_async_copy(pet_hbm.at[pid], pet_buf.at[slot, p], sem.at[slot]))
        return out

    @pl.when(g == 0)
    def _():
        for cp in copies(0, 0) + copies(1, 1):
            cp.start()

    @pl.when(j == 0)
    def _():
        m_scr[...] = jnp.full_like(m_scr, -jnp.inf)
        l_scr[...] = jnp.zeros_like(l_scr)
        acc_scr[...] = jnp.zeros_like(acc_scr)

    slot = lax.rem(g, _DECODE_SLOTS)
    for cp in copies(g, slot):
        cp.wait()
    gather = copies(jnp.minimum(g + 2, n_total - 1), lax.rem(g + 2, _DECODE_SLOTS))
    n_chunks = pages * page // chunk
    per_chunk = -(-len(gather) // n_chunks)

    q_lat = q_ref[0, :, :KV_LORA]
    q_pe = q_ref[0, :, KV_LORA:]

    def online(carry, s, values):
        m, l, acc = carry
        m_next = jnp.maximum(m, jnp.max(s, axis=1, keepdims=True))
        alpha = jnp.exp2(m - m_next)
        p = jnp.exp2(s - m_next[:, 0:1])
        l = alpha * l + jnp.sum(p, axis=1, keepdims=True)
        acc = jnp.concatenate([alpha, alpha], axis=1) * acc + _dot(p.astype(BF16), values)
        return m_next, l, acc

    def chunk_scores(c):
        lat = lat_buf[slot, pl.ds(c * chunk, chunk), :].astype(BF16)
        first = c * (chunk // page)
        pet = jnp.concatenate([pet_buf[slot, first + p] for p in range(chunk // page)], axis=1).astype(BF16)
        return lat, _dot_nt(q_lat, lat) + _dot(q_pe, pet)

    carry = (m_scr[...], l_scr[...], acc_scr[...])
    ahead = [chunk_scores(c) for c in range(min(DECODE_LOOKAHEAD, n_chunks))]
    for c in range(n_chunks):
        lat, s = ahead.pop(0)
        if c + DECODE_LOOKAHEAD < n_chunks:
            ahead.append(chunk_scores(c + DECODE_LOOKAHEAD))
        for cp in gather[c * per_chunk:(c + 1) * per_chunk]:
            cp.start()
        carry = online(carry, s, lat)
    m_scr[...], l_scr[...], acc_scr[...] = carry

    @pl.when(j == n_steps - 1)
    def _():
        ckv_new = ckvnew_ref[0].astype(BF16)
        k_new = knew_ref[0].astype(BF16)
        s = _dot_nt(q_lat, ckv_new) + _dot_nt(q_pe, k_new)
        t = lax.rem(lax.broadcasted_iota(jnp.int32, s.shape, 0), n_new)
        u = lax.broadcasted_iota(jnp.int32, s.shape, 1)
        m, l, acc = online(carry, jnp.where(u <= t, s, -jnp.inf), ckv_new)
        o_ref[0] = (acc / jnp.concatenate([l, l], axis=1)).astype(BF16)

    @pl.when(g == n_total - 1)
    def _():
        for ahead_steps in (1, 2):
            for cp in copies(g, lax.rem(g + ahead_steps, _DECODE_SLOTS)):
                cp.wait()


def _decode(page_table, q_abs, k_new, ckv_new, cache_lat, cache_pet, pages, chunk):
    nb, rows, width = q_abs.shape
    n_new = rows // MLA_HEADS
    n_pool, page, _ = cache_lat.shape
    n_pages = page_table.shape[1]
    n_steps = n_pages // pages
    assert nb * n_steps >= _DECODE_SLOTS
    tk = pages * page
    per_b = lambda n0, n1: pl.BlockSpec((1, n0, n1), lambda b, j, pt: (b, 0, 0))
    any_spec = pl.BlockSpec(memory_space=pl.ANY)
    return pl.pallas_call(
        functools.partial(_decode_kernel, pages=pages, n_steps=n_steps, page=page, n_new=n_new, chunk=chunk),
        grid_spec=pltpu.PrefetchScalarGridSpec(
            num_scalar_prefetch=1,
            grid=(nb, n_steps),
            in_specs=[per_b(rows, width), per_b(LANES, MLA_ROPE), per_b(LANES, KV_LORA), any_spec, any_spec],
            out_specs=per_b(rows, KV_LORA),
            scratch_shapes=[pltpu.VMEM((_DECODE_SLOTS, tk, KV_LORA), F32),
                            pltpu.VMEM((_DECODE_SLOTS, pages, MLA_ROPE, page), F32),
                            pltpu.SemaphoreType.DMA((_DECODE_SLOTS,)),
                            pltpu.VMEM((rows, LANES), F32),
                            pltpu.VMEM((rows, LANES), F32),
                            pltpu.VMEM((rows, KV_LORA), F32)]),
        out_shape=jax.ShapeDtypeStruct((nb, rows, KV_LORA), BF16),
        compiler_params=pltpu.CompilerParams(
            dimension_semantics=("arbitrary", "arbitrary"), vmem_limit_bytes=VMEM_LIMIT_BYTES),
        name="decode",
    )(page_table.reshape(-1), q_abs, k_new, ckv_new, cache_lat, cache_pet)


def _absorb_kernel(q_ref, wabs_ref, o_ref):
    for i in range(MLA_HEADS):
        o_ref[0, i] = _dot(q_ref[0, i], wabs_ref[i]).astype(BF16)


def _absorb(q, w_abs):
    nb, _, L, _ = q.shape
    width = w_abs.shape[-1]
    return pl.pallas_call(
        _absorb_kernel,
        grid=(nb,),
        in_specs=[pl.BlockSpec((1, MLA_HEADS, L, MLA_HEAD_PAD), lambda b: (b, 0, 0, 0)),
                  pl.BlockSpec(w_abs.shape, lambda b: (0, 0, 0))],
        out_specs=pl.BlockSpec((1, MLA_HEADS, L, width), lambda b: (b, 0, 0, 0)),
        out_shape=jax.ShapeDtypeStruct((nb, MLA_HEADS, L, width), BF16),
        compiler_params=pltpu.CompilerParams(dimension_semantics=("arbitrary",), vmem_limit_bytes=VMEM_LIMIT_BYTES),
        name="absorb",
    )(q, w_abs)


def _value_up_kernel(o_ref, sg_ref, wuv_ref, y_ref):
    y = _dot(o_ref[0], wuv_ref[0])
    for i in range(1, MLA_HEADS):
        y = y + _dot(o_ref[i], wuv_ref[i])
    y_ref[...] = (y * sg_ref[...]).astype(BF16)


def _value_up(o_lat, sg, w_uv_pad):
    _, n_tok, _ = o_lat.shape
    full = lambda a: pl.BlockSpec(a.shape, lambda i: (0,) * a.ndim)
    return pl.pallas_call(
        _value_up_kernel,
        grid=(1,),
        in_specs=[full(o_lat), full(sg), full(w_uv_pad)],
        out_specs=pl.BlockSpec((n_tok, MLA_WIDTH), lambda i: (0, 0)),
        out_shape=jax.ShapeDtypeStruct((n_tok, MLA_WIDTH), BF16),
        compiler_params=pltpu.CompilerParams(dimension_semantics=("arbitrary",), vmem_limit_bytes=VMEM_LIMIT_BYTES),
        name="value_up",
    )(o_lat, sg, w_uv_pad)


def _merge_kernel(x_ref, ret_ref, mla_ref, ng_ref, wg_ref, wbr_ref, wbm_ref, wo_ref, fg_ref, y_ref,
                  *, mla_feature_major):
    x = x_ref[0]
    D = x.shape[-1]
    h = _rms(x, ng_ref[...]).astype(BF16)
    gates = jax.nn.sigmoid(_dot(h, wg_ref[...]))
    p_mla = _dot_tn(mla_ref[0], wbm_ref[...]) if mla_feature_major else _dot(mla_ref[0], wbm_ref[...])
    merged = gates[:, :D] * _dot(ret_ref[0], wbr_ref[...]) + gates[:, D:] * p_mla
    y = x + _dot(merged.astype(BF16), wo_ref[...])
    y_ref[0] = _rms(y, fg_ref[...])


def _merge(x3, ret, mla, w, tm, mla_feature_major):
    nb, L, D = x3.shape
    tok = lambda n: pl.BlockSpec((1, tm, n), lambda b, j: (b, j, 0))
    feat = pl.BlockSpec((1, MLA_WIDTH, tm), lambda b, j: (b, 0, j))
    full = lambda a: pl.BlockSpec(a.shape, lambda b, j: (0,) * a.ndim)
    ws = [w["norm_g"], w["w_gate"], w["w_br_ret"], w["w_br_mla"], w["w_out"], w["final_g"]]
    return pl.pallas_call(
        functools.partial(_merge_kernel, mla_feature_major=mla_feature_major),
        grid=(nb, L // tm),
        in_specs=[tok(D), tok(RET_WIDTH), feat if mla_feature_major else tok(MLA_WIDTH)] + [full(a) for a in ws],
        out_specs=tok(D),
        out_shape=jax.ShapeDtypeStruct((nb, L, D), F32),
        compiler_params=pltpu.CompilerParams(
            dimension_semantics=("arbitrary", "arbitrary"), vmem_limit_bytes=VMEM_LIMIT_BYTES),
        name="merge",
    )(x3, ret, mla, *ws)


def _prep_weights(norm_gain, w_in, q_norm_gain, kv_norm_gain, w_uq, w_uk, w_uv,
                  w_branch_ret, w_branch_mla, w_out, final_norm_gain):
    D = w_in.shape[0]
    sizes = (RET_WIDTH, RET_WIDTH, RET_WIDTH, RET_WIDTH, Q_LORA, KV_LORA, MLA_ROPE, MLA_WIDTH, D, D)
    pts = np.cumsum(sizes)[:-1].tolist()
    wq_r, wk_r, wv_r, wg_r, wc_q, wc_kv, wk_pe, wg_m, wmg_r, wmg_m = jnp.split(w_in, pts, axis=1)
    place = lambda t: jnp.pad(t, ((0, 0), (MLA_NOPE, MLA_HEAD_PAD - MLA_NOPE - MLA_ROPE)))
    w_a = jnp.concatenate([wq_r, wk_r, wv_r, wg_r, wc_q, place(wk_pe), wc_kv], axis=1)
    assert w_a.shape[1] == _N_PROJ

    uq = w_uq.reshape(Q_LORA, MLA_HEADS, MLA_NOPE + MLA_ROPE)
    tail = MLA_HEAD_PAD - MLA_NOPE - MLA_ROPE
    w_uq2 = jnp.pad(uq, ((0, 0), (0, 0), (0, tail))).reshape(Q_LORA, -1)
    uk_pad = jnp.pad(w_uk, ((0, 0), (0, 0), (0, MLA_HEAD_PAD - MLA_NOPE))).reshape(KV_LORA, -1)
    uv = w_uv.reshape(KV_LORA, MLA_WIDTH)

    w_abs = jnp.zeros((MLA_HEADS, MLA_HEAD_PAD, KV_LORA + MLA_ROPE), F32)
    w_abs = w_abs.at[:, :MLA_NOPE, :KV_LORA].set(jnp.transpose(w_uk, (1, 2, 0)))
    w_abs = w_abs.at[:, MLA_NOPE:MLA_NOPE + MLA_ROPE, KV_LORA:].set(jnp.eye(MLA_ROPE, dtype=F32)[None])

    head_sel = jnp.repeat(jnp.eye(MLA_HEADS, LANES, dtype=F32), MLA_HEAD_PAD, axis=0)

    w_uv_pad = jnp.zeros((MLA_HEADS, KV_LORA, MLA_WIDTH), F32)
    for i in range(MLA_HEADS):
        w_uv_pad = w_uv_pad.at[i, :, i * MLA_V:(i + 1) * MLA_V].set(w_uv[:, i, :])

    return dict(
        head_sel=head_sel.astype(BF16), w_uv_pad=w_uv_pad.astype(BF16),
        norm_g=norm_gain.reshape(1, D), q_g=q_norm_gain.reshape(1, Q_LORA), kv_g=kv_norm_gain.reshape(1, KV_LORA),
        final_g=final_norm_gain.reshape(1, D),
        w_a=w_a.astype(BF16), w_gmt=wg_m.T.astype(BF16), w_uq=w_uq2.astype(BF16), w_uk=uk_pad.astype(BF16),
        w_uvt=uv.T.astype(BF16), w_abs=w_abs.astype(BF16),
        w_gate=jnp.concatenate([wmg_r, wmg_m], axis=1).astype(BF16),
        w_br_ret=w_branch_ret.astype(BF16), w_br_mla=w_branch_mla.astype(BF16), w_out=w_out.astype(BF16))


def _rope_table(pos):
    posf = pos.astype(F32)[:, None]

    def cos_sin(d):
        inv = jnp.power(ROPE_BASE, -jnp.arange(d // 2, dtype=F32) * (2.0 / d))
        ang = posf * inv[None, :]
        return jnp.cos(ang), jnp.sin(ang)

    rc, rs = cos_sin(RET_DK)
    mc, ms = cos_sin(MLA_ROPE)
    n = pos.shape[0]
    ones = jnp.ones((n, MLA_NOPE), F32)
    zeros = lambda width: jnp.zeros((n, width), F32)
    half = MLA_ROPE // 2
    tail = MLA_HEAD_PAD - MLA_NOPE - MLA_ROPE
    m_cos = jnp.concatenate([ones, mc, mc, zeros(tail)], axis=1)
    m_sin_down = jnp.concatenate([zeros(MLA_NOPE), -ms, zeros(half + tail)], axis=1)
    m_sin_up = jnp.concatenate([zeros(MLA_NOPE + half), ms, zeros(tail)], axis=1)
    return jnp.concatenate([rc, rc, -rs, rs, m_cos, m_sin_down, m_sin_up], axis=1)


def kernel(x_prompt, x_sample, cache_kv_latent, cache_k_rope, state_retention, page_table, norm_gain, w_in,
           q_norm_gain, kv_norm_gain, w_uq, w_uk, w_uv, w_branch_ret, w_branch_mla, w_out, final_norm_gain):
    assert norm_gain.shape[0] == 1, "single-layer kernel"
    B, S, D = x_prompt.shape
    NB, T, _ = x_sample.shape
    page = cache_kv_latent.shape[2]
    n_pages = page_table.shape[1]
    past_len = n_pages * page
    w = _prep_weights(norm_gain[0], w_in[0], q_norm_gain[0], kv_norm_gain[0], w_uq[0], w_uk[0], w_uv[0],
                      w_branch_ret[0], w_branch_mla[0], w_out[0], final_norm_gain)

    tm = min(256, S)
    qr, kr, vr, sgr, sgmt, q, k, vt, ckv, kpe, qn2, kn2 = _proj(x_prompt, _rope_table(jnp.arange(S, dtype=jnp.int32)), w, tm, BF16)
    rows = min(RET_CHUNK, S)
    ret_o, ret_state = _retention(qr, kr, vr, sgr, jnp.zeros((B, RET_HEADS, RET_DK, RET_DV), F32), B, rows)
    tile = min(1024, S)
    mla_t = _flash(q, k, vt, sgmt, qn2, kn2, tile, min(FLASH_SUB_Q, tile), min(FLASH_SUB_K, tile), FLASH_LOOKAHEAD)
    y_prompt = _merge(x_prompt, ret_o, mla_t.reshape(B, MLA_WIDTH, S), w, min(512, S), True)

    n_tok = NB * T
    tms = min(256, n_tok)
    nbs = n_tok // tms
    pos_s = past_len + jnp.arange(T, dtype=jnp.int32)
    tab_s = jnp.tile(_rope_table(pos_s), (tms // T, 1))
    qr, kr, vr, sgr, sgmt, q, _, _, ckv_s, kpe_s, _, _ = _proj(x_sample.reshape(nbs, tms, D), tab_s, w, tms, F32)
    per_seq = lambda t: t.reshape(NB, T, t.shape[-1])
    qr, kr, vr, sgr, ckv_s, kpe_s = map(per_seq, (qr, kr, vr, sgr, ckv_s, kpe_s))
    ret_o_s, ret_state_s = _retention(qr, kr, vr, sgr, state_retention[0], 8, T)
    q_abs = _absorb(q, w["w_abs"])
    q_abs = q_abs.reshape(nbs, MLA_HEADS, tms // T, T, -1).transpose(0, 2, 1, 3, 4).reshape(NB, MLA_HEADS * T, -1)
    sgm_s = sgmt.reshape(nbs, MLA_WIDTH, tms).transpose(0, 2, 1).reshape(n_tok, MLA_WIDTH)
    pad_new = lambda t: jnp.pad(t, ((0, 0), (0, LANES - T), (0, 0)))
    decode_pages = math.gcd(n_pages, DECODE_PAGES)
    o_lat = _decode(page_table, q_abs, pad_new(kpe_s), pad_new(ckv_s),
                    cache_kv_latent[0], jnp.swapaxes(cache_k_rope[0], 1, 2),
                    pages=decode_pages, chunk=min(DECODE_CHUNK, decode_pages * page))
    o_lat = o_lat.reshape(NB, MLA_HEADS, T, KV_LORA).transpose(1, 0, 2, 3).reshape(MLA_HEADS, n_tok, KV_LORA)
    mla_o_s = _value_up(o_lat, sgm_s, w["w_uv_pad"])
    y_sample = _merge(x_sample.reshape(1, n_tok, D), ret_o_s.reshape(1, n_tok, RET_WIDTH),
                      mla_o_s.reshape(1, n_tok, MLA_WIDTH), w, min(512, n_tok), False).reshape(NB, T, D)

    return (y_prompt, y_sample, ckv[None], kpe[None], ret_state[None],
            ckv_s[None], kpe_s[None], ret_state_s[None])
```

```python
import functools
import math

import jax
import jax.numpy as jnp
import numpy as np
from jax import lax
from jax.experimental import pallas as pl
from jax.experimental.pallas import tpu as pltpu

F32 = jnp.float32
BF16 = jnp.bfloat16

RET_HEADS = 4
RET_DK = 128
RET_DV = 128
RET_WIDTH = RET_HEADS * RET_DV
RET_CHUNK = 128
MLA_HEADS = 8
MLA_NOPE = 64
MLA_ROPE = 32
MLA_V = 64
MLA_WIDTH = MLA_HEADS * MLA_V
MLA_HEAD_PAD = 128
Q_LORA = 384
KV_LORA = 256
ROPE_BASE = 10000.0
EPS = 1e-6
MLA_Q_SCALE = (MLA_NOPE + MLA_ROPE) ** -0.5 * math.log2(math.e)

LANES = 128
VMEM_LIMIT_BYTES = 56 * 1024 * 1024

_OFF_QR = 0
_OFF_KR = _OFF_QR + RET_WIDTH
_OFF_VR = _OFF_KR + RET_WIDTH
_OFF_GR = _OFF_VR + RET_WIDTH
_OFF_CQ = _OFF_GR + RET_WIDTH
_OFF_KPE = _OFF_CQ + Q_LORA
_OFF_CKV = _OFF_KPE + LANES
_N_PROJ = _OFF_CKV + KV_LORA

_TAB_RC, _TAB_RS, _TAB_MC, _TAB_MS_DOWN, _TAB_MS_UP = (i * LANES for i in range(5))
_TAB_W = 5 * LANES


def _rms(x, g):
    return x * lax.rsqrt(jnp.mean(x * x, axis=-1, keepdims=True) + EPS) * g


def _silu(x):
    return x * jax.nn.sigmoid(x)


def _dot(a, b):
    return jnp.dot(a, b, preferred_element_type=F32)


def _dot_nt(a, b):
    return lax.dot_general(a, b, (((1,), (1,)), ((), ())), preferred_element_type=F32)


def _dot_tn(a, b):
    return lax.dot_general(a, b, (((0,), (0,)), ((), ())), preferred_element_type=F32)


def _proj_kernel(x_ref, tab_ref, ng_ref, wa_ref, wgmt_ref, qg_ref, kvg_ref, wuq_ref, wuk_ref, wuvt_ref, hsel_ref,
                 qr_ref, kr_ref, vr_ref, sgr_ref, sgmt_ref, q_ref, k_ref, vt_ref, ckv_ref, kpe_ref, qn2_ref, kn2_ref):
    h = _rms(x_ref[0], ng_ref[...]).astype(BF16)

    def mm(lo, n):
        return _dot(h, wa_ref[:, lo:lo + n])

    z_lat = mm(_OFF_CQ, Q_LORA + LANES)
    z_ckv = mm(_OFF_CKV, KV_LORA)
    zq = mm(_OFF_QR, RET_WIDTH)
    zk = mm(_OFF_KR, RET_WIDTH)
    zv = mm(_OFF_VR, RET_WIDTH)
    zg = mm(_OFF_GR, RET_WIDTH)
    zgm = _dot_nt(wgmt_ref[...], h)
    cq = _rms(z_lat[:, :Q_LORA], qg_ref[...]).astype(BF16)
    qq = _dot(cq, wuq_ref[...])
    ckv = _rms(z_ckv, kvg_ref[...])
    ckv_b = ckv.astype(BF16)
    k_nope = _dot(ckv_b, wuk_ref[...])
    vt = _dot_nt(wuvt_ref[...], ckv_b)

    rc = tab_ref[:, _TAB_RC:_TAB_RC + LANES]
    rs = tab_ref[:, _TAB_RS:_TAB_RS + LANES]
    mc = tab_ref[:, _TAB_MC:_TAB_MC + LANES]
    ms_down = tab_ref[:, _TAB_MS_DOWN:_TAB_MS_DOWN + LANES]
    ms_up = tab_ref[:, _TAB_MS_UP:_TAB_MS_UP + LANES]

    def ret_rope(z):
        return z * rc + pltpu.roll(z, RET_DK // 2, 1) * rs

    def mla_rope(z):
        return (z * mc + pltpu.roll(z, LANES - MLA_ROPE // 2, 1) * ms_down
                + pltpu.roll(z, MLA_ROPE // 2, 1) * ms_up)

    for i in range(RET_HEADS):
        sl = slice(i * RET_DK, (i + 1) * RET_DK)
        qr_ref[0, :, sl] = ret_rope(zq[:, sl]).astype(qr_ref.dtype)
        kr_ref[0, :, sl] = ret_rope(zk[:, sl]) * (RET_DK ** -0.5)
    vr_ref[0] = zv.astype(vr_ref.dtype)
    sgr_ref[0] = _silu(zg)
    sgmt = _silu(zgm)
    ckv_ref[0] = ckv
    kpe = mla_rope(z_lat[:, Q_LORA:])
    kpe_ref[0] = kpe[:, MLA_NOPE:MLA_NOPE + MLA_ROPE]
    q_sq, k_sq = [], []
    for i in range(MLA_HEADS):
        sl = slice(i * MLA_HEAD_PAD, (i + 1) * MLA_HEAD_PAD)
        sgmt_ref[0, i] = sgmt[i * MLA_V:(i + 1) * MLA_V, :]
        q_head = mla_rope(qq[:, sl]) * MLA_Q_SCALE
        k_head = k_nope[:, sl] + kpe
        q_ref[0, i] = q_head.astype(BF16)
        k_ref[0, i] = k_head.astype(BF16)
        vt_ref[0, i] = vt[i * MLA_V:(i + 1) * MLA_V, :].astype(BF16)
        q_sq.append((q_head * q_head).astype(BF16))
        k_sq.append((k_head * k_head).astype(BF16))
    for sq, out_ref in ((q_sq, qn2_ref), (k_sq, kn2_ref)):
        norms2 = _dot(jnp.concatenate(sq, axis=1), hsel_ref[...])
        out_ref[0, 0] = jnp.broadcast_to(jnp.max(norms2, axis=0, keepdims=True), (8, LANES))


PROJ_TOKENS = 512


def _proj(x3, tab, w, tm, ret_dtype):
    nb, L, D = x3.shape
    tok = lambda n: pl.BlockSpec((1, tm, n), lambda j, b: (b, j, 0))
    head_tok = pl.BlockSpec((1, MLA_HEADS, tm, MLA_HEAD_PAD), lambda j, b: (b, 0, j, 0))
    head_feat = pl.BlockSpec((1, MLA_HEADS, MLA_V, tm), lambda j, b: (b, 0, 0, j))
    full = lambda a: pl.BlockSpec(a.shape, lambda j, b: (0,) * a.ndim)
    sds = jax.ShapeDtypeStruct
    tok_out = lambda n, dt=F32: (tok(n), sds((nb, L, n), dt))
    outs = [tok_out(RET_WIDTH, ret_dtype), tok_out(RET_WIDTH), tok_out(RET_WIDTH, ret_dtype), tok_out(RET_WIDTH),
            (head_feat, sds((nb, MLA_HEADS, MLA_V, L), F32)),
            (head_tok, sds((nb, MLA_HEADS, L, MLA_HEAD_PAD), BF16)),
            (head_tok, sds((nb, MLA_HEADS, L, MLA_HEAD_PAD), BF16)),
            (head_feat, sds((nb, MLA_HEADS, MLA_V, L), BF16)),
            tok_out(KV_LORA), tok_out(MLA_ROPE)]
    norm_spec = pl.BlockSpec((1, 1, 8, LANES), lambda j, b: (b, j, 0, 0))
    outs += [(norm_spec, sds((nb, L // tm, 8, LANES), F32))] * 2
    ws = [w["norm_g"], w["w_a"], w["w_gmt"], w["q_g"], w["kv_g"], w["w_uq"], w["w_uk"], w["w_uvt"], w["head_sel"]]
    return pl.pallas_call(
        _proj_kernel,
        grid=(L // tm, nb),
        in_specs=[tok(D), pl.BlockSpec((tm, _TAB_W), lambda j, b: (j, 0))] + [full(a) for a in ws],
        out_specs=[o[0] for o in outs],
        out_shape=[o[1] for o in outs],
        compiler_params=pltpu.CompilerParams(
            dimension_semantics=("arbitrary", "arbitrary"), vmem_limit_bytes=VMEM_LIMIT_BYTES),
        name="proj",
    )(x3, tab, *ws)


def _ret_kernel(q_ref, k_ref, v_ref, sg_ref, st_in_ref, dec_ref, qd_ref, kd_ref, sd_ref,
                o_ref, st_out_ref, st_scr, pad_scr, *, group, rows, chunk, per_step):
    c = pl.program_id(1)

    @pl.when(c == 0)
    def _():
        st_scr[...] = st_in_ref[...]
        pad_scr[...] = jnp.zeros_like(pad_scr)

    units = [(n, g, i) for n in range(per_step) for g in range(group) for i in range(RET_HEADS)]

    def operand(ref, n, g, sl, slot):
        if rows == chunk:
            return ref[g, n * rows:(n + 1) * rows, sl]
        idx = (g * RET_HEADS + sl.start // RET_DK) * 3 + slot
        pad_scr[idx, 0:rows, :] = ref[g, :, sl].astype(F32)
        return pad_scr[idx]

    def products(n, g, i):
        sl = slice(i * RET_DK, (i + 1) * RET_DK)
        q = operand(q_ref, n, g, sl, 0).astype(BF16)
        k = operand(k_ref, n, g, sl, 1)
        v = operand(v_ref, n, g, sl, 2).astype(BF16)
        state = st_scr[g, i]
        return q, k, v, state, _dot_nt(q, k.astype(BF16)), _dot(q, state.astype(BF16))

    assert RET_LOOKAHEAD < group * RET_HEADS or per_step == 1
    ahead = [products(*u) for u in units[:RET_LOOKAHEAD]]
    for t, (n, g, i) in enumerate(units):
        sl = slice(i * RET_DK, (i + 1) * RET_DK)
        tok = slice(n * rows, (n + 1) * rows)
        q, k, v, state, a, cross = ahead.pop(0)
        if t + RET_LOOKAHEAD < len(units):
            ahead.append(products(*units[t + RET_LOOKAHEAD]))
        o = _dot((a * dec_ref[i]).astype(BF16), v) + cross * qd_ref[i]
        o = o * lax.rsqrt(jnp.mean(o * o, axis=-1, keepdims=True) + EPS)
        o_ref[g, tok, sl] = (o[0:rows] * sg_ref[g, tok, sl]).astype(BF16)
        kd = (k * kd_ref[i]).astype(BF16)
        st_scr[g, i] = sd_ref[i] * state + _dot_tn(kd, v)

    @pl.when(c == pl.num_programs(1) - 1)
    def _():
        st_out_ref[...] = st_scr[...]


RET_CHUNKS_PER_STEP = 2
RET_LOOKAHEAD = 3


def _ret_tables(rows, chunk):
    lg = jnp.log1p(-jnp.exp2(-5.0 - jnp.arange(RET_HEADS, dtype=F32)))
    idx = jnp.arange(rows, dtype=F32)
    diff = idx[:, None] - idx[None, :]
    decay = jnp.where(diff[None] >= 0, jnp.exp(jnp.maximum(diff, 0.0)[None] * lg[:, None, None]), 0.0)
    q_decay = jnp.exp((idx[None, :] + 1.0) * lg[:, None])
    k_decay = jnp.exp((rows - 1.0 - idx)[None, :] * lg[:, None])
    s_decay = jnp.exp(rows * lg)
    pad = chunk - rows
    decay = jnp.pad(decay, ((0, 0), (0, pad), (0, pad)))
    bcast = lambda t: jnp.broadcast_to(jnp.pad(t, ((0, 0), (0, pad)))[:, :, None], (RET_HEADS, chunk, LANES))
    s_decay = jnp.broadcast_to(s_decay[:, None, None], (RET_HEADS, RET_DK, LANES))
    return decay, bcast(q_decay), bcast(k_decay), s_decay


def _retention(q, k, v, sg, state, group, rows):
    nb, L, _ = q.shape
    chunk = max(rows, RET_CHUNK)
    per_step = math.gcd(L // rows, RET_CHUNKS_PER_STEP) if rows == chunk else 1
    dec, qd, kd, sd = _ret_tables(rows, chunk)
    tok = pl.BlockSpec((group, rows * per_step, RET_WIDTH), lambda b, c: (b, c, 0))
    st = pl.BlockSpec((group, RET_HEADS, RET_DK, RET_DV), lambda b, c: (b, 0, 0, 0))
    full = lambda a: pl.BlockSpec(a.shape, lambda b, c: (0,) * a.ndim)
    return pl.pallas_call(
        functools.partial(_ret_kernel, group=group, rows=rows, chunk=chunk, per_step=per_step),
        grid=(nb // group, L // (rows * per_step)),
        in_specs=[tok, tok, tok, tok, st, full(dec), full(qd), full(kd), full(sd)],
        out_specs=[tok, st],
        out_shape=[jax.ShapeDtypeStruct((nb, L, RET_WIDTH), BF16),
                   jax.ShapeDtypeStruct(state.shape, F32)],
        scratch_shapes=[pltpu.VMEM((group, RET_HEADS, RET_DK, RET_DV), F32),
                        pltpu.VMEM((3 * group * RET_HEADS if rows < chunk else 3, chunk, RET_DK), F32)],
        compiler_params=pltpu.CompilerParams(
            dimension_semantics=("arbitrary", "arbitrary"), vmem_limit_bytes=VMEM_LIMIT_BYTES),
        name="retention",
    )(q, k, v, sg, state, dec, qd, kd, sd)


def _flash_kernel(qi_ref, ki_ref, q_ref, k_ref, vt_ref, sgt_ref, qn2_ref, kn2_ref, o_ref,
                  m_scr, l_scr, acc_scr, mmin_scr, *, tile, sub_q, sub_k, lookahead):
    step = pl.program_id(1)
    qi = qi_ref[step]
    ki = ki_ref[step]

    bound2 = (jnp.max(qn2_ref[0], axis=0) * jnp.max(kn2_ref[0], axis=0))[0:1] * FLASH_BOUND_SLACK
    lane = lax.broadcasted_iota(jnp.int32, (1, LANES), 1)

    def no_head_exceeds(limit):
        bad = jnp.where(limit > 0.0, jnp.where(bound2 <= limit * limit, 0.0, 1.0), 1.0)
        return jnp.sum(jnp.where(lane < MLA_HEADS, bad, 0.0)) == 0.0

    @pl.when(ki == 0)
    def _():
        start = jnp.where(no_head_exceeds(jnp.full((1, LANES), FLASH_STALE_MARGIN, F32)), 0.0, -jnp.inf)
        m_scr[...] = jnp.full(m_scr.shape, start, F32)
        mmin_scr[...] = jnp.full(mmin_scr.shape, start, F32)
        l_scr[...] = jnp.zeros_like(l_scr)
        acc_scr[...] = jnp.zeros_like(acc_scr)

    keep_stabiliser = no_head_exceeds(mmin_scr[...] + FLASH_STALE_MARGIN)

    def accumulate(diagonal, stale, heads_per_iter):
        units = [(hh, a, b) for hh in range(heads_per_iter)
                 for a in range(tile // sub_q) for b in range(tile // sub_k)
                 if not diagonal or b * sub_k <= (a + 1) * sub_q - 1]

        def head_group(g, carry):
            def scores(hh, a, b):
                i = g * heads_per_iter + hh
                return _dot_nt(k_ref[0, i, pl.ds(b * sub_k, sub_k), :], q_ref[0, i, pl.ds(a * sub_q, sub_q), :])

            ahead = [scores(*u) for u in units[:lookahead]]
            for t, (hh, a, b) in enumerate(units):
                i = g * heads_per_iter + hh
                s = ahead.pop(0)
                if t + lookahead < len(units):
                    ahead.append(scores(*units[t + lookahead]))
                qs = pl.ds(a * sub_q, sub_q)
                if t == 0 or units[t - 1][:2] != (hh, a):
                    m, l, acc = m_scr[i, :, qs], l_scr[i, :, qs], acc_scr[i, :, qs]
                if diagonal and (b + 1) * sub_k - 1 > a * sub_q:
                    key = b * sub_k + lax.broadcasted_iota(jnp.int32, (sub_k, sub_q), 0)
                    query = a * sub_q + lax.broadcasted_iota(jnp.int32, (sub_k, sub_q), 1)
                    s = jnp.where(key <= query, s, -jnp.inf)
                if not stale:
                    m_next = jnp.maximum(m, jnp.max(s, axis=0, keepdims=True))
                    alpha = jnp.exp2(m - m_next)
                    l, acc, m = alpha * l, alpha * acc, m_next
                p = jnp.exp2(s - m).astype(BF16)
                vt1 = jnp.concatenate([vt_ref[0, i, :, pl.ds(b * sub_k, sub_k)], jnp.ones((16, sub_k), BF16)], axis=0)
                pv = _dot(vt1, p)
                l = l + pv[MLA_V:MLA_V + 1]
                acc = acc + pv[:MLA_V]
                if t + 1 == len(units) or units[t + 1][:2] != (hh, a):
                    l_scr[i, :, qs], acc_scr[i, :, qs] = l, acc
                    if not stale:
                        m_scr[i, :, qs] = m
            return carry

        lax.fori_loop(0, MLA_HEADS // heads_per_iter, head_group, 0)

    for diagonal in (False, True):
        on_tile = (ki == qi) if diagonal else (ki < qi)

        @pl.when(jnp.logical_and(on_tile, keep_stabiliser))
        def _():
            accumulate(diagonal, True, FLASH_HEADS_PER_ITER)

        @pl.when(jnp.logical_and(on_tile, jnp.logical_not(keep_stabiliser)))
        def _():
            accumulate(diagonal, False, FLASH_HEADS_PER_ITER_EXACT)
            smallest = jnp.full((1, LANES), jnp.inf, F32)
            for i in range(MLA_HEADS):
                smallest = jnp.where(lane == i, jnp.min(m_scr[i], axis=1, keepdims=True), smallest)
            mmin_scr[...] = smallest

    @pl.when(ki == qi)
    def _():
        for i in range(MLA_HEADS):
            o_ref[0, i] = (acc_scr[i] / l_scr[i] * sgt_ref[0, i]).astype(BF16)


FLASH_SUB_Q = 512
FLASH_SUB_K = 512
FLASH_LOOKAHEAD = 2
FLASH_HEADS_PER_ITER = 4
FLASH_HEADS_PER_ITER_EXACT = 2
FLASH_STALE_MARGIN = 60.0
FLASH_BOUND_SLACK = 1.05


def _flash(q, k, vt, sgt, qn2, kn2, tile, sub_q, sub_k, lookahead):
    nb, _, S, _ = q.shape
    nt = S // tile
    norms_per_tile = kn2.shape[1] // nt
    pairs = [(a, b) for a in range(nt) for b in range(a + 1)]
    qi = jnp.asarray(np.array([p[0] for p in pairs], np.int32))
    ki = jnp.asarray(np.array([p[1] for p in pairs], np.int32))
    q_tok = pl.BlockSpec((1, MLA_HEADS, tile, MLA_HEAD_PAD), lambda b, s, qi, ki: (b, 0, qi[s], 0))
    k_tok = pl.BlockSpec((1, MLA_HEADS, tile, MLA_HEAD_PAD), lambda b, s, qi, ki: (b, 0, ki[s], 0))
    k_feat = pl.BlockSpec((1, MLA_HEADS, MLA_V, tile), lambda b, s, qi, ki: (b, 0, 0, ki[s]))
    q_feat = pl.BlockSpec((1, MLA_HEADS, MLA_V, tile), lambda b, s, qi, ki: (b, 0, 0, qi[s]))
    q_norm = pl.BlockSpec((1, norms_per_tile, 8, LANES), lambda b, s, qi, ki: (b, qi[s], 0, 0))
    k_norm = pl.BlockSpec((1, norms_per_tile, 8, LANES), lambda b, s, qi, ki: (b, ki[s], 0, 0))
    return pl.pallas_call(
        functools.partial(_flash_kernel, tile=tile, sub_q=sub_q, sub_k=sub_k, lookahead=lookahead),
        grid_spec=pltpu.PrefetchScalarGridSpec(
            num_scalar_prefetch=2,
            grid=(nb, len(pairs)),
            in_specs=[q_tok, k_tok, k_feat, q_feat, q_norm, k_norm],
            out_specs=q_feat,
            scratch_shapes=[pltpu.VMEM((MLA_HEADS, 1, tile), F32),
                            pltpu.VMEM((MLA_HEADS, 1, tile), F32),
                            pltpu.VMEM((MLA_HEADS, MLA_V, tile), F32),
                            pltpu.VMEM((1, LANES), F32)]),
        out_shape=jax.ShapeDtypeStruct((nb, MLA_HEADS, MLA_V, S), BF16),
        compiler_params=pltpu.CompilerParams(
            dimension_semantics=("arbitrary", "arbitrary"), vmem_limit_bytes=VMEM_LIMIT_BYTES),
        name="flash",
    )(qi, ki, q, k, vt, sgt, qn2, kn2)


_DECODE_SLOTS = 3
DECODE_PAGES = 64
DECODE_CHUNK = 2048
DECODE_LOOKAHEAD = 3


def _decode_kernel(pt_ref, q_ref, knew_ref, ckvnew_ref, lat_hbm, pet_hbm,
                   o_ref, lat_buf, pet_buf, sem, m_scr, l_scr, acc_scr, newlat_scr, newpe_scr,
                   *, pages, n_steps, page, n_new, chunk):
    b = pl.program_id(0)
    j = pl.program_id(1)
    n_total = pl.num_programs(0) * n_steps
    g = b * n_steps + j

    def copies(step, slot):
        out = []
        for p in range(pages):
            pid = pt_ref[step * pages + p]
            out.append(pltpu.make_async_copy(
                lat_hbm.at[pid], lat_buf.at[slot, pl.ds(p * page, page)], sem.at[slot]))
            out.append(pltpu.make_async_copy(pet_hbm.at[pid], pet_buf.at[slot, p], sem.at[slot]))
        return out

    @pl.when(g == 0)
    def _():
        for cp in copies(0, 0) + copies(1, 1):
            cp.start()
        newlat_scr[...] = jnp.zeros_like(newlat_scr)
        newpe_scr[...] = jnp.zeros_like(newpe_scr)

    @pl.when(j == 0)
    def _():
        m_scr[...] = jnp.full_like(m_scr, -jnp.inf)
        l_scr[...] = jnp.zeros_like(l_scr)
        acc_scr[...] = jnp.zeros_like(acc_scr)

    slot = lax.rem(g, _DECODE_SLOTS)
    for cp in copies(g, slot):
        cp.wait()
    gather = copies(jnp.minimum(g + 2, n_total - 1), lax.rem(g + 2, _DECODE_SLOTS))
    n_chunks = pages * page // chunk
    per_chunk = -(-len(gather) // n_chunks)

    q_lat = q_ref[0, :, :KV_LORA]
    q_pe = q_ref[0, :, KV_LORA:]

    def online(carry, s, values):
        m, l, acc = carry
        m_next = jnp.maximum(m, jnp.max(s, axis=1, keepdims=True))
        alpha = jnp.exp2(m - m_next)
        p = jnp.exp2(s - m_next[:, 0:1])
        l = alpha * l + jnp.sum(p, axis=1, keepdims=True)
        acc = jnp.concatenate([alpha, alpha], axis=1) * acc + _dot(p.astype(BF16), values)
        return m_next, l, acc

    def chunk_scores(c):
        lat = lat_buf[slot, pl.ds(c * chunk, chunk), :].astype(BF16)
        first = c * (chunk // page)
        pet = jnp.concatenate([pet_buf[slot, first + p] for p in range(chunk // page)], axis=1).astype(BF16)
        return lat, _dot_nt(q_lat, lat) + _dot(q_pe, pet)

    carry = (m_scr[...], l_scr[...], acc_scr[...])
    ahead = [chunk_scores(c) for c in range(min(DECODE_LOOKAHEAD, n_chunks))]
    for c in range(n_chunks):
        lat, s = ahead.pop(0)
        if c + DECODE_LOOKAHEAD < n_chunks:
            ahead.append(chunk_scores(c + DECODE_LOOKAHEAD))
        for cp in gather[c * per_chunk:(c + 1) * per_chunk]:
            cp.start()
        carry = online(carry, s, lat)
    m_scr[...], l_scr[...], acc_scr[...] = carry

    @pl.when(j == n_steps - 1)
    def _():
        newlat_scr[0:n_new, :] = ckvnew_ref[0]
        newpe_scr[0:n_new, :] = knew_ref[0]
        ckv_new = newlat_scr[...].astype(BF16)
        k_new = newpe_scr[...].astype(BF16)
        s = _dot_nt(q_lat, ckv_new) + _dot_nt(q_pe, k_new)
        t = lax.rem(lax.broadcasted_iota(jnp.int32, s.shape, 0), n_new)
        u = lax.broadcasted_iota(jnp.int32, s.shape, 1)
        m, l, acc = online(carry, jnp.where(u <= t, s, -jnp.inf), ckv_new)
        o_ref[0] = (acc / jnp.concatenate([l, l], axis=1)).astype(BF16)

    @pl.when(g == n_total - 1)
    def _():
        for ahead_steps in (1, 2):
            for cp in copies(g, lax.rem(g + ahead_steps, _DECODE_SLOTS)):
                cp.wait()


def _decode(page_table, q_abs, k_new, ckv_new, cache_lat, cache_pet, pages, chunk):
    nb, rows, width = q_abs.shape
    n_new = rows // MLA_HEADS
    n_pool, page, _ = cache_lat.shape
    n_pages = page_table.shape[1]
    n_steps = n_pages // pages
    assert nb * n_steps >= _DECODE_SLOTS
    tk = pages * page
    per_b = lambda n0, n1: pl.BlockSpec((1, n0, n1), lambda b, j, pt: (b, 0, 0))
    any_spec = pl.BlockSpec(memory_space=pl.ANY)
    return pl.pallas_call(
        functools.partial(_decode_kernel, pages=pages, n_steps=n_steps, page=page, n_new=n_new, chunk=chunk),
        grid_spec=pltpu.PrefetchScalarGridSpec(
            num_scalar_prefetch=1,
            grid=(nb, n_steps),
            in_specs=[per_b(rows, width), per_b(n_new, MLA_ROPE), per_b(n_new, KV_LORA), any_spec, any_spec],
            out_specs=per_b(rows, KV_LORA),
            scratch_shapes=[pltpu.VMEM((_DECODE_SLOTS, tk, KV_LORA), F32),
                            pltpu.VMEM((_DECODE_SLOTS, pages, MLA_ROPE, page), F32),
                            pltpu.SemaphoreType.DMA((_DECODE_SLOTS,)),
                            pltpu.VMEM((rows, LANES), F32),
                            pltpu.VMEM((rows, LANES), F32),
                            pltpu.VMEM((rows, KV_LORA), F32),
                            pltpu.VMEM((LANES, KV_LORA), F32),
                            pltpu.VMEM((LANES, MLA_ROPE), F32)]),
        out_shape=jax.ShapeDtypeStruct((nb, rows, KV_LORA), BF16),
        compiler_params=pltpu.CompilerParams(
            dimension_semantics=("arbitrary", "arbitrary"), vmem_limit_bytes=VMEM_LIMIT_BYTES),
        name="decode",
    )(page_table.reshape(-1), q_abs, k_new, ckv_new, cache_lat, cache_pet)


def _absorb_kernel(q_ref, wabs_ref, o_ref):
    for i in range(MLA_HEADS):
        o_ref[0, i] = _dot(q_ref[0, i], wabs_ref[i]).astype(BF16)


def _absorb(q, w_abs):
    nb, _, L, _ = q.shape
    width = w_abs.shape[-1]
    return pl.pallas_call(
        _absorb_kernel,
        grid=(nb,),
        in_specs=[pl.BlockSpec((1, MLA_HEADS, L, MLA_HEAD_PAD), lambda b: (b, 0, 0, 0)),
                  pl.BlockSpec(w_abs.shape, lambda b: (0, 0, 0))],
        out_specs=pl.BlockSpec((1, MLA_HEADS, L, width), lambda b: (b, 0, 0, 0)),
        out_shape=jax.ShapeDtypeStruct((nb, MLA_HEADS, L, width), BF16),
        compiler_params=pltpu.CompilerParams(dimension_semantics=("arbitrary",), vmem_limit_bytes=VMEM_LIMIT_BYTES),
        name="absorb",
    )(q, w_abs)


def _value_up_kernel(o_ref, sg_ref, wuv_ref, y_ref):
    y = _dot(o_ref[0], wuv_ref[0])
    for i in range(1, MLA_HEADS):
        y = y + _dot(o_ref[i], wuv_ref[i])
    y_ref[...] = (y * sg_ref[...]).astype(BF16)


def _value_up(o_lat, sg, w_uv_pad):
    _, n_tok, _ = o_lat.shape
    full = lambda a: pl.BlockSpec(a.shape, lambda i: (0,) * a.ndim)
    return pl.pallas_call(
        _value_up_kernel,
        grid=(1,),
        in_specs=[full(o_lat), full(sg), full(w_uv_pad)],
        out_specs=pl.BlockSpec((n_tok, MLA_WIDTH), lambda i: (0, 0)),
        out_shape=jax.ShapeDtypeStruct((n_tok, MLA_WIDTH), BF16),
        compiler_params=pltpu.CompilerParams(dimension_semantics=("arbitrary",), vmem_limit_bytes=VMEM_LIMIT_BYTES),
        name="value_up",
    )(o_lat, sg, w_uv_pad)


def _merge_kernel(x_ref, ret_ref, mla_ref, ng_ref, wg_ref, wbr_ref, wbm_ref, wo_ref, fg_ref, y_ref,
                  *, mla_feature_major):
    x = x_ref[0]
    D = x.shape[-1]
    h = _rms(x, ng_ref[...]).astype(BF16)
    gates = jax.nn.sigmoid(_dot(h, wg_ref[...]))
    p_mla = _dot_tn(mla_ref[0], wbm_ref[...]) if mla_feature_major else _dot(mla_ref[0], wbm_ref[...])
    merged = gates[:, :D] * _dot(ret_ref[0], wbr_ref[...]) + gates[:, D:] * p_mla
    y = x + _dot(merged.astype(BF16), wo_ref[...])
    y_ref[0] = _rms(y, fg_ref[...])


def _merge(x3, ret, mla, w, tm, mla_feature_major):
    nb, L, D = x3.shape
    tok = lambda n: pl.BlockSpec((1, tm, n), lambda b, j: (b, j, 0))
    feat = pl.BlockSpec((1, MLA_WIDTH, tm), lambda b, j: (b, 0, j))
    full = lambda a: pl.BlockSpec(a.shape, lambda b, j: (0,) * a.ndim)
    ws = [w["norm_g"], w["w_gate"], w["w_br_ret"], w["w_br_mla"], w["w_out"], w["final_g"]]
    return pl.pallas_call(
        functools.partial(_merge_kernel, mla_feature_major=mla_feature_major),
        grid=(nb, L // tm),
        in_specs=[tok(D), tok(RET_WIDTH), feat if mla_feature_major else tok(MLA_WIDTH)] + [full(a) for a in ws],
        out_specs=tok(D),
        out_shape=jax.ShapeDtypeStruct((nb, L, D), F32),
        compiler_params=pltpu.CompilerParams(
            dimension_semantics=("arbitrary", "arbitrary"), vmem_limit_bytes=VMEM_LIMIT_BYTES),
        name="merge",
    )(x3, ret, mla, *ws)


def _prep_weights(norm_gain, w_in, q_norm_gain, kv_norm_gain, w_uq, w_uk, w_uv,
                  w_branch_ret, w_branch_mla, w_out, final_norm_gain):
    D = w_in.shape[0]
    sizes = (RET_WIDTH, RET_WIDTH, RET_WIDTH, RET_WIDTH, Q_LORA, KV_LORA, MLA_ROPE, MLA_WIDTH, D, D)
    pts = np.cumsum(sizes)[:-1].tolist()
    wq_r, wk_r, wv_r, wg_r, wc_q, wc_kv, wk_pe, wg_m, wmg_r, wmg_m = jnp.split(w_in, pts, axis=1)
    place = lambda t: jnp.pad(t, ((0, 0), (MLA_NOPE, MLA_HEAD_PAD - MLA_NOPE - MLA_ROPE)))
    w_a = jnp.concatenate([wq_r, wk_r, wv_r, wg_r, wc_q, place(wk_pe), wc_kv], axis=1)
    assert w_a.shape[1] == _N_PROJ

    uq = w_uq.reshape(Q_LORA, MLA_HEADS, MLA_NOPE + MLA_ROPE)
    tail = MLA_HEAD_PAD - MLA_NOPE - MLA_ROPE
    w_uq2 = jnp.pad(uq, ((0, 0), (0, 0), (0, tail))).reshape(Q_LORA, -1)
    uk_pad = jnp.pad(w_uk, ((0, 0), (0, 0), (0, MLA_HEAD_PAD - MLA_NOPE))).reshape(KV_LORA, -1)
    uv = w_uv.reshape(KV_LORA, MLA_WIDTH)

    w_abs = jnp.zeros((MLA_HEADS, MLA_HEAD_PAD, KV_LORA + MLA_ROPE), F32)
    w_abs = w_abs.at[:, :MLA_NOPE, :KV_LORA].set(jnp.transpose(w_uk, (1, 2, 0)))
    w_abs = w_abs.at[:, MLA_NOPE:MLA_NOPE + MLA_ROPE, KV_LORA:].set(jnp.eye(MLA_ROPE, dtype=F32)[None])

    head_sel = jnp.repeat(jnp.eye(MLA_HEADS, LANES, dtype=F32), MLA_HEAD_PAD, axis=0)

    w_uv_pad = jnp.zeros((MLA_HEADS, KV_LORA, MLA_WIDTH), F32)
    for i in range(MLA_HEADS):
        w_uv_pad = w_uv_pad.at[i, :, i * MLA_V:(i + 1) * MLA_V].set(w_uv[:, i, :])

    return dict(
        head_sel=head_sel.astype(BF16), w_uv_pad=w_uv_pad.astype(BF16),
        norm_g=norm_gain.reshape(1, D), q_g=q_norm_gain.reshape(1, Q_LORA), kv_g=kv_norm_gain.reshape(1, KV_LORA),
        final_g=final_norm_gain.reshape(1, D),
        w_a=w_a.astype(BF16), w_gmt=wg_m.T.astype(BF16), w_uq=w_uq2.astype(BF16), w_uk=uk_pad.astype(BF16),
        w_uvt=uv.T.astype(BF16), w_abs=w_abs.astype(BF16),
        w_gate=jnp.concatenate([wmg_r, wmg_m], axis=1).astype(BF16),
        w_br_ret=w_branch_ret.astype(BF16), w_br_mla=w_branch_mla.astype(BF16), w_out=w_out.astype(BF16))


def _rope_table(pos):
    posf = pos.astype(F32)[:, None]
    lane = np.arange(LANES)

    def inv_freq(d):
        return jnp.power(ROPE_BASE, -jnp.arange(d // 2, dtype=F32) * (2.0 / d))

    half = MLA_ROPE // 2
    ret_ang = posf * jnp.tile(inv_freq(RET_DK), 2)[None, :]
    ret_sign = np.where(lane < RET_DK // 2, -1.0, 1.0).astype(np.float32)
    in_rope = (lane >= MLA_NOPE) & (lane < MLA_NOPE + MLA_ROPE)
    mla_freq = jnp.where(in_rope, jnp.tile(inv_freq(MLA_ROPE), LANES // half), 0.0)
    mla_ang = posf * mla_freq[None, :]
    cos_keep = (lane < MLA_NOPE + MLA_ROPE).astype(np.float32)
    down = np.where((lane >= MLA_NOPE) & (lane < MLA_NOPE + half), -1.0, 0.0).astype(np.float32)
    up = np.where((lane >= MLA_NOPE + half) & (lane < MLA_NOPE + MLA_ROPE), 1.0, 0.0).astype(np.float32)
    mla_sin = jnp.sin(mla_ang)
    return jnp.concatenate([jnp.cos(ret_ang), jnp.sin(ret_ang) * ret_sign, jnp.cos(mla_ang) * cos_keep,
                            mla_sin * down, mla_sin * up], axis=1)


def kernel(x_prompt, x_sample, cache_kv_latent, cache_k_rope, state_retention, page_table, norm_gain, w_in,
           q_norm_gain, kv_norm_gain, w_uq, w_uk, w_uv, w_branch_ret, w_branch_mla, w_out, final_norm_gain):
    assert norm_gain.shape[0] == 1, "single-layer kernel"
    B, S, D = x_prompt.shape
    NB, T, _ = x_sample.shape
    page = cache_kv_latent.shape[2]
    n_pages = page_table.shape[1]
    past_len = n_pages * page
    w = _prep_weights(norm_gain[0], w_in[0], q_norm_gain[0], kv_norm_gain[0], w_uq[0], w_uk[0], w_uv[0],
                      w_branch_ret[0], w_branch_mla[0], w_out[0], final_norm_gain)

    tm = min(PROJ_TOKENS, S)
    qr, kr, vr, sgr, sgmt, q, k, vt, ckv, kpe, qn2, kn2 = _proj(x_prompt, _rope_table(jnp.arange(S, dtype=jnp.int32)), w, tm, BF16)
    rows = min(RET_CHUNK, S)
    ret_o, ret_state = _retention(qr, kr, vr, sgr, jnp.zeros((B, RET_HEADS, RET_DK, RET_DV), F32), B, rows)
    tile = min(1024, S)
    mla_t = _flash(q, k, vt, sgmt, qn2, kn2, tile, min(FLASH_SUB_Q, tile), min(FLASH_SUB_K, tile), FLASH_LOOKAHEAD)
    y_prompt = _merge(x_prompt, ret_o, mla_t.reshape(B, MLA_WIDTH, S), w, min(512, S), True)

    n_tok = NB * T
    tms = min(256, n_tok)
    nbs = n_tok // tms
    pos_s = past_len + jnp.arange(T, dtype=jnp.int32)
    tab_s = jnp.tile(_rope_table(pos_s), (tms // T, 1))
    qr, kr, vr, sgr, sgmt, q, _, _, ckv_s, kpe_s, _, _ = _proj(x_sample.reshape(nbs, tms, D), tab_s, w, tms, F32)
    per_seq = lambda t: t.reshape(NB, T, t.shape[-1])
    qr, kr, vr, sgr, ckv_s, kpe_s = map(per_seq, (qr, kr, vr, sgr, ckv_s, kpe_s))
    ret_o_s, ret_state_s = _retention(qr, kr, vr, sgr, state_retention[0], 8, T)
    q_abs = _absorb(q, w["w_abs"])
    q_abs = q_abs.reshape(nbs, MLA_HEADS, tms // T, T, -1).transpose(0, 2, 1, 3, 4).reshape(NB, MLA_HEADS * T, -1)
    sgm_s = sgmt.reshape(nbs, MLA_WIDTH, tms).transpose(0, 2, 1).reshape(n_tok, MLA_WIDTH)
    decode_pages = math.gcd(n_pages, DECODE_PAGES)
    o_lat = _decode(page_table, q_abs, kpe_s, ckv_s,
                    cache_kv_latent[0], jnp.swapaxes(cache_k_rope[0], 1, 2),
                    pages=decode_pages, chunk=min(DECODE_CHUNK, decode_pages * page))
    o_lat = o_lat.reshape(NB, MLA_HEADS, T, KV_LORA).transpose(1, 0, 2, 3).reshape(MLA_HEADS, n_tok, KV_LORA)
    mla_o_s = _value_up(o_lat, sgm_s, w["w_uv_pad"])
    y_sample = _merge(x_sample.reshape(1, n_tok, D), ret_o_s.reshape(1, n_tok, RET_WIDTH),
                      mla_o_s.reshape(1, n_tok, MLA_WIDTH), w, min(512, n_tok), False).reshape(NB, T, D)

    return (y_prompt, y_sample, ckv[None], kpe[None], ret_state[None],
            ckv_s[None], kpe_s[None], ret_state_s[None])
```

```python
import functools
import math

import jax
import jax.numpy as jnp
import numpy as np
from jax import lax
from jax.experimental import pallas as pl
from jax.experimental.pallas import tpu as pltpu

F32 = jnp.float32
BF16 = jnp.bfloat16

RET_HEADS = 4
RET_DK = 128
RET_DV = 128
RET_WIDTH = RET_HEADS * RET_DV
RET_CHUNK = 128
MLA_HEADS = 8
MLA_NOPE = 64
MLA_ROPE = 32
MLA_V = 64
MLA_WIDTH = MLA_HEADS * MLA_V
MLA_HEAD_PAD = 128
Q_LORA = 384
KV_LORA = 256
ROPE_BASE = 10000.0
EPS = 1e-6
MLA_Q_SCALE = (MLA_NOPE + MLA_ROPE) ** -0.5 * math.log2(math.e)

LANES = 128
VMEM_LIMIT_BYTES = 56 * 1024 * 1024

_OFF_QR = 0
_OFF_KR = _OFF_QR + RET_WIDTH
_OFF_VR = _OFF_KR + RET_WIDTH
_OFF_GR = _OFF_VR + RET_WIDTH
_OFF_CQ = _OFF_GR + RET_WIDTH
_OFF_KPE = _OFF_CQ + Q_LORA
_OFF_CKV = _OFF_KPE + LANES
_N_PROJ = _OFF_CKV + KV_LORA

_TAB_RC, _TAB_RS, _TAB_MC, _TAB_MS_DOWN, _TAB_MS_UP = (i * LANES for i in range(5))
_TAB_W = 5 * LANES


def _rms(x, g):
    return x * lax.rsqrt(jnp.mean(x * x, axis=-1, keepdims=True) + EPS) * g


def _silu(x):
    return x * jax.nn.sigmoid(x)


def _dot(a, b):
    return jnp.dot(a, b, preferred_element_type=F32)


def _dot_nt(a, b):
    return lax.dot_general(a, b, (((1,), (1,)), ((), ())), preferred_element_type=F32)


def _dot_tn(a, b):
    return lax.dot_general(a, b, (((0,), (0,)), ((), ())), preferred_element_type=F32)


def _proj_kernel(x_ref, tab_ref, ng_ref, wa_ref, wgmt_ref, qg_ref, kvg_ref, wuq_ref, wuk_ref, wuvt_ref, hsel_ref,
                 qr_ref, kr_ref, vr_ref, sgr_ref, sgmt_ref, q_ref, k_ref, vt_ref, ckv_ref, kpe_ref, qn2_ref, kn2_ref):
    h = _rms(x_ref[0], ng_ref[...]).astype(BF16)

    def mm(lo, n):
        return _dot(h, wa_ref[:, lo:lo + n])

    z_lat = mm(_OFF_CQ, Q_LORA + LANES)
    z_ckv = mm(_OFF_CKV, KV_LORA)
    zq = mm(_OFF_QR, RET_WIDTH)
    zk = mm(_OFF_KR, RET_WIDTH)
    zv = mm(_OFF_VR, RET_WIDTH)
    zg = mm(_OFF_GR, RET_WIDTH)
    zgm = _dot_nt(wgmt_ref[...], h)
    cq = _rms(z_lat[:, :Q_LORA], qg_ref[...]).astype(BF16)
    qq = _dot(cq, wuq_ref[...])
    ckv = _rms(z_ckv, kvg_ref[...])
    ckv_b = ckv.astype(BF16)
    k_nope = _dot(ckv_b, wuk_ref[...])
    vt = _dot_nt(wuvt_ref[...], ckv_b)

    rc = tab_ref[:, _TAB_RC:_TAB_RC + LANES]
    rs = tab_ref[:, _TAB_RS:_TAB_RS + LANES]
    mc = tab_ref[:, _TAB_MC:_TAB_MC + LANES]
    ms_down = tab_ref[:, _TAB_MS_DOWN:_TAB_MS_DOWN + LANES]
    ms_up = tab_ref[:, _TAB_MS_UP:_TAB_MS_UP + LANES]

    def ret_rope(z):
        return z * rc + pltpu.roll(z, RET_DK // 2, 1) * rs

    def mla_rope(z):
        return (z * mc + pltpu.roll(z, LANES - MLA_ROPE // 2, 1) * ms_down
                + pltpu.roll(z, MLA_ROPE // 2, 1) * ms_up)

    for i in range(RET_HEADS):
        sl = slice(i * RET_DK, (i + 1) * RET_DK)
        qr_ref[0, :, sl] = ret_rope(zq[:, sl]).astype(qr_ref.dtype)
        kr_ref[0, :, sl] = ret_rope(zk[:, sl]) * (RET_DK ** -0.5)
    vr_ref[0] = zv.astype(vr_ref.dtype)
    sgr_ref[0] = _silu(zg)
    sgmt = _silu(zgm)
    ckv_ref[0] = ckv
    kpe = mla_rope(z_lat[:, Q_LORA:])
    kpe_ref[0] = kpe[:, MLA_NOPE:MLA_NOPE + MLA_ROPE]
    q_sq, k_sq = [], []
    for i in range(MLA_HEADS):
        sl = slice(i * MLA_HEAD_PAD, (i + 1) * MLA_HEAD_PAD)
        sgmt_ref[0, i] = sgmt[i * MLA_V:(i + 1) * MLA_V, :]
        q_head = mla_rope(qq[:, sl]) * MLA_Q_SCALE
        k_head = k_nope[:, sl] + kpe
        q_ref[0, i] = q_head.astype(BF16)
        k_ref[0, i] = k_head.astype(BF16)
        vt_ref[0, i] = vt[i * MLA_V:(i + 1) * MLA_V, :].astype(BF16)
        q_sq.append((q_head * q_head).astype(BF16))
        k_sq.append((k_head * k_head).astype(BF16))
    for sq, out_ref in ((q_sq, qn2_ref), (k_sq, kn2_ref)):
        norms2 = _dot(jnp.concatenate(sq, axis=1), hsel_ref[...])
        out_ref[0, 0] = jnp.broadcast_to(jnp.max(norms2, axis=0, keepdims=True), (8, LANES))


PROJ_TOKENS = 512


def _proj(x3, tab, w, tm, ret_dtype):
    nb, L, D = x3.shape
    tok = lambda n: pl.BlockSpec((1, tm, n), lambda j, b: (b, j, 0))
    head_tok = pl.BlockSpec((1, MLA_HEADS, tm, MLA_HEAD_PAD), lambda j, b: (b, 0, j, 0))
    head_feat = pl.BlockSpec((1, MLA_HEADS, MLA_V, tm), lambda j, b: (b, 0, 0, j))
    full = lambda a: pl.BlockSpec(a.shape, lambda j, b: (0,) * a.ndim)
    sds = jax.ShapeDtypeStruct
    tok_out = lambda n, dt=F32: (tok(n), sds((nb, L, n), dt))
    outs = [tok_out(RET_WIDTH, ret_dtype), tok_out(RET_WIDTH), tok_out(RET_WIDTH, ret_dtype), tok_out(RET_WIDTH),
            (head_feat, sds((nb, MLA_HEADS, MLA_V, L), F32)),
            (head_tok, sds((nb, MLA_HEADS, L, MLA_HEAD_PAD), BF16)),
            (head_tok, sds((nb, MLA_HEADS, L, MLA_HEAD_PAD), BF16)),
            (head_feat, sds((nb, MLA_HEADS, MLA_V, L), BF16)),
            tok_out(KV_LORA), tok_out(MLA_ROPE)]
    norm_spec = pl.BlockSpec((1, 1, 8, LANES), lambda j, b: (b, j, 0, 0))
    outs += [(norm_spec, sds((nb, L // tm, 8, LANES), F32))] * 2
    ws = [w["norm_g"], w["w_a"], w["w_gmt"], w["q_g"], w["kv_g"], w["w_uq"], w["w_uk"], w["w_uvt"], w["head_sel"]]
    return pl.pallas_call(
        _proj_kernel,
        grid=(L // tm, nb),
        in_specs=[tok(D), pl.BlockSpec((tm, _TAB_W), lambda j, b: (j, 0))] + [full(a) for a in ws],
        out_specs=[o[0] for o in outs],
        out_shape=[o[1] for o in outs],
        compiler_params=pltpu.CompilerParams(
            dimension_semantics=("arbitrary", "arbitrary"), vmem_limit_bytes=VMEM_LIMIT_BYTES),
        name="proj",
    )(x3, tab, *ws)


def _ret_kernel(q_ref, k_ref, v_ref, sg_ref, st_in_ref, dec_ref, qd_ref, kd_ref, sd_ref,
                o_ref, st_out_ref, st_scr, pad_scr, *, group, rows, chunk, per_step):
    c = pl.program_id(1)

    @pl.when(c == 0)
    def _():
        st_scr[...] = st_in_ref[...]
        pad_scr[...] = jnp.zeros_like(pad_scr)

    units = [(n, g, i) for n in range(per_step) for g in range(group) for i in range(RET_HEADS)]

    def operand(ref, n, g, sl, slot):
        if rows == chunk:
            return ref[g, n * rows:(n + 1) * rows, sl]
        idx = (g * RET_HEADS + sl.start // RET_DK) * 3 + slot
        pad_scr[idx, 0:rows, :] = ref[g, :, sl].astype(F32)
        return pad_scr[idx]

    def products(n, g, i):
        sl = slice(i * RET_DK, (i + 1) * RET_DK)
        q = operand(q_ref, n, g, sl, 0).astype(BF16)
        k = operand(k_ref, n, g, sl, 1)
        v = operand(v_ref, n, g, sl, 2).astype(BF16)
        state = st_scr[g, i]
        return q, k, v, state, _dot_nt(q, k.astype(BF16)), _dot(q, state.astype(BF16))

    assert RET_LOOKAHEAD < group * RET_HEADS or per_step == 1
    ahead = [products(*u) for u in units[:RET_LOOKAHEAD]]
    for t, (n, g, i) in enumerate(units):
        sl = slice(i * RET_DK, (i + 1) * RET_DK)
        tok = slice(n * rows, (n + 1) * rows)
        q, k, v, state, a, cross = ahead.pop(0)
        if t + RET_LOOKAHEAD < len(units):
            ahead.append(products(*units[t + RET_LOOKAHEAD]))
        o = _dot((a * dec_ref[i]).astype(BF16), v) + cross * qd_ref[i]
        o = o * lax.rsqrt(jnp.mean(o * o, axis=-1, keepdims=True) + EPS)
        o_ref[g, tok, sl] = (o[0:rows] * sg_ref[g, tok, sl]).astype(BF16)
        kd = (k * kd_ref[i]).astype(BF16)
        st_scr[g, i] = sd_ref[i] * state + _dot_tn(kd, v)

    @pl.when(c == pl.num_programs(1) - 1)
    def _():
        st_out_ref[...] = st_scr[...]


RET_CHUNKS_PER_STEP = 2
RET_LOOKAHEAD = 3


def _ret_tables(rows, chunk):
    lg = jnp.log1p(-jnp.exp2(-5.0 - jnp.arange(RET_HEADS, dtype=F32)))
    idx = jnp.arange(rows, dtype=F32)
    diff = idx[:, None] - idx[None, :]
    decay = jnp.where(diff[None] >= 0, jnp.exp(jnp.maximum(diff, 0.0)[None] * lg[:, None, None]), 0.0)
    q_decay = jnp.exp((idx[None, :] + 1.0) * lg[:, None])
    k_decay = jnp.exp((rows - 1.0 - idx)[None, :] * lg[:, None])
    s_decay = jnp.exp(rows * lg)
    pad = chunk - rows
    decay = jnp.pad(decay, ((0, 0), (0, pad), (0, pad)))
    bcast = lambda t: jnp.broadcast_to(jnp.pad(t, ((0, 0), (0, pad)))[:, :, None], (RET_HEADS, chunk, LANES))
    s_decay = jnp.broadcast_to(s_decay[:, None, None], (RET_HEADS, RET_DK, LANES))
    return decay, bcast(q_decay), bcast(k_decay), s_decay


def _retention(q, k, v, sg, state, group, rows):
    nb, L, _ = q.shape
    chunk = max(rows, RET_CHUNK)
    per_step = math.gcd(L // rows, RET_CHUNKS_PER_STEP) if rows == chunk else 1
    dec, qd, kd, sd = _ret_tables(rows, chunk)
    tok = pl.BlockSpec((group, rows * per_step, RET_WIDTH), lambda b, c: (b, c, 0))
    st = pl.BlockSpec((group, RET_HEADS, RET_DK, RET_DV), lambda b, c: (b, 0, 0, 0))
    full = lambda a: pl.BlockSpec(a.shape, lambda b, c: (0,) * a.ndim)
    return pl.pallas_call(
        functools.partial(_ret_kernel, group=group, rows=rows, chunk=chunk, per_step=per_step),
        grid=(nb // group, L // (rows * per_step)),
        in_specs=[tok, tok, tok, tok, st, full(dec), full(qd), full(kd), full(sd)],
        out_specs=[tok, st],
        out_shape=[jax.ShapeDtypeStruct((nb, L, RET_WIDTH), BF16),
                   jax.ShapeDtypeStruct(state.shape, F32)],
        scratch_shapes=[pltpu.VMEM((group, RET_HEADS, RET_DK, RET_DV), F32),
                        pltpu.VMEM((3 * group * RET_HEADS if rows < chunk else 3, chunk, RET_DK), F32)],
        compiler_params=pltpu.CompilerParams(
            dimension_semantics=("arbitrary", "arbitrary"), vmem_limit_bytes=VMEM_LIMIT_BYTES),
        name="retention",
    )(q, k, v, sg, state, dec, qd, kd, sd)


def _flash_kernel(qi_ref, ki_ref, q_ref, k_ref, vt_ref, sgt_ref, qn2_ref, kn2_ref, o_ref,
                  m_scr, l_scr, acc_scr, mmin_scr, *, tile, sub_q, sub_k, lookahead):
    step = pl.program_id(1)
    qi = qi_ref[step]
    ki = ki_ref[step]

    bound2 = (jnp.max(qn2_ref[0], axis=0) * jnp.max(kn2_ref[0], axis=0))[0:1] * FLASH_BOUND_SLACK
    lane = lax.broadcasted_iota(jnp.int32, (1, LANES), 1)

    def no_head_exceeds(limit):
        bad = jnp.where(limit > 0.0, jnp.where(bound2 <= limit * limit, 0.0, 1.0), 1.0)
        return jnp.sum(jnp.where(lane < MLA_HEADS, bad, 0.0)) == 0.0

    @pl.when(ki == 0)
    def _():
        start = jnp.where(no_head_exceeds(jnp.full((1, LANES), FLASH_STALE_MARGIN, F32)), 0.0, -jnp.inf)
        m_scr[...] = jnp.full(m_scr.shape, start, F32)
        mmin_scr[...] = jnp.full(mmin_scr.shape, start, F32)
        l_scr[...] = jnp.zeros_like(l_scr)
        acc_scr[...] = jnp.zeros_like(acc_scr)

    keep_stabiliser = no_head_exceeds(mmin_scr[...] + FLASH_STALE_MARGIN)

    def accumulate(diagonal, stale, heads_per_iter):
        units = [(hh, a, b) for hh in range(heads_per_iter)
                 for a in range(tile // sub_q) for b in range(tile // sub_k)
                 if not diagonal or b * sub_k <= (a + 1) * sub_q - 1]

        def head_group(g, carry):
            def scores(hh, a, b):
                i = g * heads_per_iter + hh
                return _dot_nt(k_ref[0, i, pl.ds(b * sub_k, sub_k), :], q_ref[0, i, pl.ds(a * sub_q, sub_q), :])

            ahead = [scores(*u) for u in units[:lookahead]]
            for t, (hh, a, b) in enumerate(units):
                i = g * heads_per_iter + hh
                s = ahead.pop(0)
                if t + lookahead < len(units):
                    ahead.append(scores(*units[t + lookahead]))
                qs = pl.ds(a * sub_q, sub_q)
                if t == 0 or units[t - 1][:2] != (hh, a):
                    m, l, acc = m_scr[i, :, qs], l_scr[i, :, qs], acc_scr[i, :, qs]
                if diagonal and (b + 1) * sub_k - 1 > a * sub_q:
                    key = b * sub_k + lax.broadcasted_iota(jnp.int32, (sub_k, sub_q), 0)
                    query = a * sub_q + lax.broadcasted_iota(jnp.int32, (sub_k, sub_q), 1)
                    s = jnp.where(key <= query, s, -jnp.inf)
                if not stale:
                    m_next = jnp.maximum(m, jnp.max(s, axis=0, keepdims=True))
                    alpha = jnp.exp2(m - m_next)
                    l, acc, m = alpha * l, alpha * acc, m_next
                p = jnp.exp2(s - m).astype(BF16)
                vt1 = jnp.concatenate([vt_ref[0, i, :, pl.ds(b * sub_k, sub_k)], jnp.ones((16, sub_k), BF16)], axis=0)
                pv = _dot(vt1, p)
                l = l + pv[MLA_V:MLA_V + 1]
                acc = acc + pv[:MLA_V]
                if t + 1 == len(units) or units[t + 1][:2] != (hh, a):
                    l_scr[i, :, qs], acc_scr[i, :, qs] = l, acc
                    if not stale:
                        m_scr[i, :, qs] = m
            return carry

        lax.fori_loop(0, MLA_HEADS // heads_per_iter, head_group, 0)

    for diagonal in (False, True):
        on_tile = (ki == qi) if diagonal else (ki < qi)

        @pl.when(jnp.logical_and(on_tile, keep_stabiliser))
        def _():
            accumulate(diagonal, True, FLASH_HEADS_PER_ITER)

        @pl.when(jnp.logical_and(on_tile, jnp.logical_not(keep_stabiliser)))
        def _():
            accumulate(diagonal, False, FLASH_HEADS_PER_ITER_EXACT)
            smallest = jnp.full((1, LANES), jnp.inf, F32)
            for i in range(MLA_HEADS):
                smallest = jnp.where(lane == i, jnp.min(m_scr[i], axis=1, keepdims=True), smallest)
            mmin_scr[...] = smallest

    @pl.when(ki == qi)
    def _():
        for i in range(MLA_HEADS):
            o_ref[0, i] = (acc_scr[i] / l_scr[i] * sgt_ref[0, i]).astype(BF16)


FLASH_SUB_Q = 512
FLASH_SUB_K = 512
FLASH_LOOKAHEAD = 2
FLASH_HEADS_PER_ITER = 4
FLASH_HEADS_PER_ITER_EXACT = 2
FLASH_STALE_MARGIN = 60.0
FLASH_BOUND_SLACK = 1.05


def _flash(q, k, vt, sgt, qn2, kn2, tile, sub_q, sub_k, lookahead):
    nb, _, S, _ = q.shape
    nt = S // tile
    norms_per_tile = kn2.shape[1] // nt
    pairs = [(a, b) for a in range(nt) for b in range(a + 1)]
    qi = jnp.asarray(np.array([p[0] for p in pairs], np.int32))
    ki = jnp.asarray(np.array([p[1] for p in pairs], np.int32))
    q_tok = pl.BlockSpec((1, MLA_HEADS, tile, MLA_HEAD_PAD), lambda b, s, qi, ki: (b, 0, qi[s], 0))
    k_tok = pl.BlockSpec((1, MLA_HEADS, tile, MLA_HEAD_PAD), lambda b, s, qi, ki: (b, 0, ki[s], 0))
    k_feat = pl.BlockSpec((1, MLA_HEADS, MLA_V, tile), lambda b, s, qi, ki: (b, 0, 0, ki[s]))
    q_feat = pl.BlockSpec((1, MLA_HEADS, MLA_V, tile), lambda b, s, qi, ki: (b, 0, 0, qi[s]))
    q_norm = pl.BlockSpec((1, norms_per_tile, 8, LANES), lambda b, s, qi, ki: (b, qi[s], 0, 0))
    k_norm = pl.BlockSpec((1, norms_per_tile, 8, LANES), lambda b, s, qi, ki: (b, ki[s], 0, 0))
    return pl.pallas_call(
        functools.partial(_flash_kernel, tile=tile, sub_q=sub_q, sub_k=sub_k, lookahead=lookahead),
        grid_spec=pltpu.PrefetchScalarGridSpec(
            num_scalar_prefetch=2,
            grid=(nb, len(pairs)),
            in_specs=[q_tok, k_tok, k_feat, q_feat, q_norm, k_norm],
            out_specs=q_feat,
            scratch_shapes=[pltpu.VMEM((MLA_HEADS, 1, tile), F32),
                            pltpu.VMEM((MLA_HEADS, 1, tile), F32),
                            pltpu.VMEM((MLA_HEADS, MLA_V, tile), F32),
                            pltpu.VMEM((1, LANES), F32)]),
        out_shape=jax.ShapeDtypeStruct((nb, MLA_HEADS, MLA_V, S), BF16),
        compiler_params=pltpu.CompilerParams(
            dimension_semantics=("arbitrary", "arbitrary"), vmem_limit_bytes=VMEM_LIMIT_BYTES),
        name="flash",
    )(qi, ki, q, k, vt, sgt, qn2, kn2)


_DECODE_SLOTS = 3
DECODE_PAGES = 64
DECODE_CHUNK = 2048
DECODE_LOOKAHEAD = 3


def _decode_kernel(pt_ref, q_ref, knew_ref, ckvnew_ref, lat_hbm, pet_hbm,
                   o_ref, lat_buf, pet_buf, sem, m_scr, l_scr, acc_scr, newlat_scr, newpe_scr,
                   *, pages, n_steps, page, n_new, chunk):
    b = pl.program_id(0)
    j = pl.program_id(1)
    n_total = pl.num_programs(0) * n_steps
    g = b * n_steps + j

    def copies(step, slot):
        out = []
        for p in range(pages):
            pid = pt_ref[step * pages + p]
            out.append(pltpu.make_async_copy(
                lat_hbm.at[pid], lat_buf.at[slot, pl.ds(p * page, page)], sem.at[slot]))
            out.append(pltpu.make_async_copy(pet_hbm.at[pid], pet_buf.at[slot, p], sem.at[slot]))
        return out

    @pl.when(g == 0)
    def _():
        for cp in copies(0, 0) + copies(1, 1):
            cp.start()
        newlat_scr[...] = jnp.zeros_like(newlat_scr)
        newpe_scr[...] = jnp.zeros_like(newpe_scr)

    @pl.when(j == 0)
    def _():
        m_scr[...] = jnp.full_like(m_scr, -jnp.inf)
        l_scr[...] = jnp.zeros_like(l_scr)
        acc_scr[...] = jnp.zeros_like(acc_scr)

    slot = lax.rem(g, _DECODE_SLOTS)
    for cp in copies(g, slot):
        cp.wait()
    gather = copies(jnp.minimum(g + 2, n_total - 1), lax.rem(g + 2, _DECODE_SLOTS))
    n_chunks = pages * page // chunk
    per_chunk = -(-len(gather) // n_chunks)

    q_lat = q_ref[0, :, :KV_LORA]
    q_pe = q_ref[0, :, KV_LORA:]

    def online(carry, s, values):
        m, l, acc = carry
        m_next = jnp.maximum(m, jnp.max(s, axis=1, keepdims=True))
        alpha = jnp.exp2(m - m_next)
        p = jnp.exp2(s - m_next[:, 0:1])
        l = alpha * l + jnp.sum(p, axis=1, keepdims=True)
        acc = jnp.concatenate([alpha, alpha], axis=1) * acc + _dot(p.astype(BF16), values)
        return m_next, l, acc

    def chunk_scores(c):
        lat = lat_buf[slot, pl.ds(c * chunk, chunk), :].astype(BF16)
        first = c * (chunk // page)
        pet = jnp.concatenate([pet_buf[slot, first + p] for p in range(chunk // page)], axis=1).astype(BF16)
        return lat, _dot_nt(q_lat, lat) + _dot(q_pe, pet)

    carry = (m_scr[...], l_scr[...], acc_scr[...])
    ahead = [chunk_scores(c) for c in range(min(DECODE_LOOKAHEAD, n_chunks))]
    for c in range(n_chunks):
        lat, s = ahead.pop(0)
        if c + DECODE_LOOKAHEAD < n_chunks:
            ahead.append(chunk_scores(c + DECODE_LOOKAHEAD))
        for cp in gather[c * per_chunk:(c + 1) * per_chunk]:
            cp.start()
        carry = online(carry, s, lat)
    m_scr[...], l_scr[...], acc_scr[...] = carry

    @pl.when(j == n_steps - 1)
    def _():
        newlat_scr[0:n_new, :] = ckvnew_ref[0]
        newpe_scr[0:n_new, :] = knew_ref[0]
        ckv_new = newlat_scr[...].astype(BF16)
        k_new = newpe_scr[...].astype(BF16)
        s = _dot_nt(q_lat, ckv_new) + _dot_nt(q_pe, k_new)
        t = lax.rem(lax.broadcasted_iota(jnp.int32, s.shape, 0), n_new)
        u = lax.broadcasted_iota(jnp.int32, s.shape, 1)
        m, l, acc = online(carry, jnp.where(u <= t, s, -jnp.inf), ckv_new)
        o_ref[0] = (acc / jnp.concatenate([l, l], axis=1)).astype(BF16)

    @pl.when(g == n_total - 1)
    def _():
        for ahead_steps in (1, 2):
            for cp in copies(g, lax.rem(g + ahead_steps, _DECODE_SLOTS)):
                cp.wait()


def _decode(page_table, q_abs, k_new, ckv_new, cache_lat, cache_pet, pages, chunk):
    nb, rows, width = q_abs.shape
    n_new = rows // MLA_HEADS
    n_pool, page, _ = cache_lat.shape
    n_pages = page_table.shape[1]
    n_steps = n_pages // pages
    assert nb * n_steps >= _DECODE_SLOTS
    tk = pages * page
    per_b = lambda n0, n1: pl.BlockSpec((1, n0, n1), lambda b, j, pt: (b, 0, 0))
    any_spec = pl.BlockSpec(memory_space=pl.ANY)
    return pl.pallas_call(
        functools.partial(_decode_kernel, pages=pages, n_steps=n_steps, page=page, n_new=n_new, chunk=chunk),
        grid_spec=pltpu.PrefetchScalarGridSpec(
            num_scalar_prefetch=1,
            grid=(nb, n_steps),
            in_specs=[per_b(rows, width), per_b(n_new, MLA_ROPE), per_b(n_new, KV_LORA), any_spec, any_spec],
            out_specs=per_b(rows, KV_LORA),
            scratch_shapes=[pltpu.VMEM((_DECODE_SLOTS, tk, KV_LORA), F32),
                            pltpu.VMEM((_DECODE_SLOTS, pages, MLA_ROPE, page), F32),
                            pltpu.SemaphoreType.DMA((_DECODE_SLOTS,)),
                            pltpu.VMEM((rows, LANES), F32),
                            pltpu.VMEM((rows, LANES), F32),
                            pltpu.VMEM((rows, KV_LORA), F32),
                            pltpu.VMEM((LANES, KV_LORA), F32),
                            pltpu.VMEM((LANES, MLA_ROPE), F32)]),
        out_shape=jax.ShapeDtypeStruct((nb, rows, KV_LORA), BF16),
        compiler_params=pltpu.CompilerParams(
            dimension_semantics=("arbitrary", "arbitrary"), vmem_limit_bytes=VMEM_LIMIT_BYTES),
        name="decode",
    )(page_table.reshape(-1), q_abs, k_new, ckv_new, cache_lat, cache_pet)


def _absorb_kernel(q_ref, wabs_ref, o_ref):
    for i in range(MLA_HEADS):
        o_ref[0, i] = _dot(q_ref[0, i], wabs_ref[i]).astype(BF16)


def _absorb(q, w_abs):
    nb, _, L, _ = q.shape
    width = w_abs.shape[-1]
    return pl.pallas_call(
        _absorb_kernel,
        grid=(nb,),
        in_specs=[pl.BlockSpec((1, MLA_HEADS, L, MLA_HEAD_PAD), lambda b: (b, 0, 0, 0)),
                  pl.BlockSpec(w_abs.shape, lambda b: (0, 0, 0))],
        out_specs=pl.BlockSpec((1, MLA_HEADS, L, width), lambda b: (b, 0, 0, 0)),
        out_shape=jax.ShapeDtypeStruct((nb, MLA_HEADS, L, width), BF16),
        compiler_params=pltpu.CompilerParams(dimension_semantics=("arbitrary",), vmem_limit_bytes=VMEM_LIMIT_BYTES),
        name="absorb",
    )(q, w_abs)


def _value_up_kernel(o_ref, sg_ref, wuv_ref, y_ref):
    y = _dot(o_ref[0], wuv_ref[0])
    for i in range(1, MLA_HEADS):
        y = y + _dot(o_ref[i], wuv_ref[i])
    y_ref[...] = (y * sg_ref[...]).astype(BF16)


def _value_up(o_lat, sg, w_uv_pad):
    _, n_tok, _ = o_lat.shape
    full = lambda a: pl.BlockSpec(a.shape, lambda i: (0,) * a.ndim)
    return pl.pallas_call(
        _value_up_kernel,
        grid=(1,),
        in_specs=[full(o_lat), full(sg), full(w_uv_pad)],
        out_specs=pl.BlockSpec((n_tok, MLA_WIDTH), lambda i: (0, 0)),
        out_shape=jax.ShapeDtypeStruct((n_tok, MLA_WIDTH), BF16),
        compiler_params=pltpu.CompilerParams(dimension_semantics=("arbitrary",), vmem_limit_bytes=VMEM_LIMIT_BYTES),
        name="value_up",
    )(o_lat, sg, w_uv_pad)


def _merge_kernel(x_ref, ret_ref, mla_ref, ng_ref, wg_ref, wbr_ref, wbm_ref, wo_ref, fg_ref, y_ref,
                  *, mla_feature_major):
    x = x_ref[0]
    D = x.shape[-1]
    h = _rms(x, ng_ref[...]).astype(BF16)
    gates = jax.nn.sigmoid(_dot(h, wg_ref[...]))
    p_mla = _dot_tn(mla_ref[0], wbm_ref[...]) if mla_feature_major else _dot(mla_ref[0], wbm_ref[...])
    merged = gates[:, :D] * _dot(ret_ref[0], wbr_ref[...]) + gates[:, D:] * p_mla
    y = x + _dot(merged.astype(BF16), wo_ref[...])
    y_ref[0] = _rms(y, fg_ref[...])


def _merge(x3, ret, mla, w, tm, mla_feature_major):
    nb, L, D = x3.shape
    tok = lambda n: pl.BlockSpec((1, tm, n), lambda b, j: (b, j, 0))
    feat = pl.BlockSpec((1, MLA_WIDTH, tm), lambda b, j: (b, 0, j))
    full = lambda a: pl.BlockSpec(a.shape, lambda b, j: (0,) * a.ndim)
    ws = [w["norm_g"], w["w_gate"], w["w_br_ret"], w["w_br_mla"], w["w_out"], w["final_g"]]
    return pl.pallas_call(
        functools.partial(_merge_kernel, mla_feature_major=mla_feature_major),
        grid=(nb, L // tm),
        in_specs=[tok(D), tok(RET_WIDTH), feat if mla_feature_major else tok(MLA_WIDTH)] + [full(a) for a in ws],
        out_specs=tok(D),
        out_shape=jax.ShapeDtypeStruct((nb, L, D), F32),
        compiler_params=pltpu.CompilerParams(
            dimension_semantics=("arbitrary", "arbitrary"), vmem_limit_bytes=VMEM_LIMIT_BYTES),
        name="merge",
    )(x3, ret, mla, *ws)


def _prep_weights(norm_gain, w_in, q_norm_gain, kv_norm_gain, w_uq, w_uk, w_uv,
                  w_branch_ret, w_branch_mla, w_out, final_norm_gain):
    D = w_in.shape[0]
    sizes = (RET_WIDTH, RET_WIDTH, RET_WIDTH, RET_WIDTH, Q_LORA, KV_LORA, MLA_ROPE, MLA_WIDTH, D, D)
    pts = np.cumsum(sizes)[:-1].tolist()
    wq_r, wk_r, wv_r, wg_r, wc_q, wc_kv, wk_pe, wg_m, wmg_r, wmg_m = jnp.split(w_in, pts, axis=1)
    place = lambda t: jnp.pad(t, ((0, 0), (MLA_NOPE, MLA_HEAD_PAD - MLA_NOPE - MLA_ROPE)))
    w_a = jnp.concatenate([wq_r, wk_r, wv_r, wg_r, wc_q, place(wk_pe), wc_kv], axis=1)
    assert w_a.shape[1] == _N_PROJ

    uq = w_uq.reshape(Q_LORA, MLA_HEADS, MLA_NOPE + MLA_ROPE)
    tail = MLA_HEAD_PAD - MLA_NOPE - MLA_ROPE
    w_uq2 = jnp.pad(uq, ((0, 0), (0, 0), (0, tail))).reshape(Q_LORA, -1)
    uk_pad = jnp.pad(w_uk, ((0, 0), (0, 0), (0, MLA_HEAD_PAD - MLA_NOPE))).reshape(KV_LORA, -1)
    uv = w_uv.reshape(KV_LORA, MLA_WIDTH)

    w_abs = jnp.zeros((MLA_HEADS, MLA_HEAD_PAD, KV_LORA + MLA_ROPE), F32)
    w_abs = w_abs.at[:, :MLA_NOPE, :KV_LORA].set(jnp.transpose(w_uk, (1, 2, 0)))
    w_abs = w_abs.at[:, MLA_NOPE:MLA_NOPE + MLA_ROPE, KV_LORA:].set(jnp.eye(MLA_ROPE, dtype=F32)[None])

    head_sel = jnp.repeat(jnp.eye(MLA_HEADS, LANES, dtype=F32), MLA_HEAD_PAD, axis=0)

    w_uv_pad = jnp.zeros((MLA_HEADS, KV_LORA, MLA_WIDTH), F32)
    for i in range(MLA_HEADS):
        w_uv_pad = w_uv_pad.at[i, :, i * MLA_V:(i + 1) * MLA_V].set(w_uv[:, i, :])

    return dict(
        head_sel=head_sel.astype(BF16), w_uv_pad=w_uv_pad.astype(BF16),
        norm_g=norm_gain.reshape(1, D), q_g=q_norm_gain.reshape(1, Q_LORA), kv_g=kv_norm_gain.reshape(1, KV_LORA),
        final_g=final_norm_gain.reshape(1, D),
        w_a=w_a.astype(BF16), w_gmt=wg_m.T.astype(BF16), w_uq=w_uq2.astype(BF16), w_uk=uk_pad.astype(BF16),
        w_uvt=uv.T.astype(BF16), w_abs=w_abs.astype(BF16),
        w_gate=jnp.concatenate([wmg_r, wmg_m], axis=1).astype(BF16),
        w_br_ret=w_branch_ret.astype(BF16), w_br_mla=w_branch_mla.astype(BF16), w_out=w_out.astype(BF16))


ROPE_SPLIT = 64


def _rope_table(pos):
    n = pos.shape[0]
    lane = np.arange(LANES)

    def inv_freq(d):
        return jnp.power(ROPE_BASE, -jnp.arange(d // 2, dtype=F32) * (2.0 / d))

    def cos_sin(freq):
        if n % ROPE_SPLIT:
            ang = pos.astype(F32)[:, None] * freq[None, :]
            return jnp.cos(ang), jnp.sin(ang)
        grid = pos.reshape(n // ROPE_SPLIT, ROPE_SPLIT)
        hi = grid[:, :1].astype(F32) * freq[None, :]
        lo = (grid[:1, :] - grid[:1, :1]).astype(F32).T * freq[None, :]
        ch, sh, cl, sl = jnp.cos(hi)[:, None, :], jnp.sin(hi)[:, None, :], jnp.cos(lo)[None], jnp.sin(lo)[None]
        return (ch * cl - sh * sl).reshape(n, LANES), (sh * cl + ch * sl).reshape(n, LANES)

    half = MLA_ROPE // 2
    ret_cos, ret_sin = cos_sin(jnp.tile(inv_freq(RET_DK), 2))
    ret_sign = np.where(lane < RET_DK // 2, -1.0, 1.0).astype(np.float32)
    in_rope = (lane >= MLA_NOPE) & (lane < MLA_NOPE + MLA_ROPE)
    mla_cos, mla_sin = cos_sin(jnp.where(in_rope, jnp.tile(inv_freq(MLA_ROPE), LANES // half), 0.0))
    cos_keep = (lane < MLA_NOPE + MLA_ROPE).astype(np.float32)
    down = np.where((lane >= MLA_NOPE) & (lane < MLA_NOPE + half), -1.0, 0.0).astype(np.float32)
    up = np.where((lane >= MLA_NOPE + half) & (lane < MLA_NOPE + MLA_ROPE), 1.0, 0.0).astype(np.float32)
    return jnp.concatenate([ret_cos, ret_sin * ret_sign, mla_cos * cos_keep, mla_sin * down, mla_sin * up], axis=1)


def kernel(x_prompt, x_sample, cache_kv_latent, cache_k_rope, state_retention, page_table, norm_gain, w_in,
           q_norm_gain, kv_norm_gain, w_uq, w_uk, w_uv, w_branch_ret, w_branch_mla, w_out, final_norm_gain):
    assert norm_gain.shape[0] == 1, "single-layer kernel"
    B, S, D = x_prompt.shape
    NB, T, _ = x_sample.shape
    page = cache_kv_latent.shape[2]
    n_pages = page_table.shape[1]
    past_len = n_pages * page
    w = _prep_weights(norm_gain[0], w_in[0], q_norm_gain[0], kv_norm_gain[0], w_uq[0], w_uk[0], w_uv[0],
                      w_branch_ret[0], w_branch_mla[0], w_out[0], final_norm_gain)

    tm = min(PROJ_TOKENS, S)
    qr, kr, vr, sgr, sgmt, q, k, vt, ckv, kpe, qn2, kn2 = _proj(x_prompt, _rope_table(jnp.arange(S, dtype=jnp.int32)), w, tm, BF16)
    rows = min(RET_CHUNK, S)
    ret_o, ret_state = _retention(qr, kr, vr, sgr, jnp.zeros((B, RET_HEADS, RET_DK, RET_DV), F32), B, rows)
    tile = min(1024, S)
    mla_t = _flash(q, k, vt, sgmt, qn2, kn2, tile, min(FLASH_SUB_Q, tile), min(FLASH_SUB_K, tile), FLASH_LOOKAHEAD)
    y_prompt = _merge(x_prompt, ret_o, mla_t.reshape(B, MLA_WIDTH, S), w, min(512, S), True)

    n_tok = NB * T
    tms = min(256, n_tok)
    nbs = n_tok // tms
    pos_s = past_len + jnp.arange(T, dtype=jnp.int32)
    tab_s = jnp.tile(_rope_table(pos_s), (tms // T, 1))
    qr, kr, vr, sgr, sgmt, q, _, _, ckv_s, kpe_s, _, _ = _proj(x_sample.reshape(nbs, tms, D), tab_s, w, tms, F32)
    per_seq = lambda t: t.reshape(NB, T, t.shape[-1])
    qr, kr, vr, sgr, ckv_s, kpe_s = map(per_seq, (qr, kr, vr, sgr, ckv_s, kpe_s))
    ret_o_s, ret_state_s = _retention(qr, kr, vr, sgr, state_retention[0], 8, T)
    q_abs = _absorb(q, w["w_abs"])
    q_abs = q_abs.reshape(nbs, MLA_HEADS, tms // T, T, -1).transpose(0, 2, 1, 3, 4).reshape(NB, MLA_HEADS * T, -1)
    sgm_s = sgmt.reshape(nbs, MLA_WIDTH, tms).transpose(0, 2, 1).reshape(n_tok, MLA_WIDTH)
    decode_pages = math.gcd(n_pages, DECODE_PAGES)
    o_lat = _decode(page_table, q_abs, kpe_s, ckv_s,
                    cache_kv_latent[0], jnp.swapaxes(cache_k_rope[0], 1, 2),
                    pages=decode_pages, chunk=min(DECODE_CHUNK, decode_pages * page))
    o_lat = o_lat.reshape(NB, MLA_HEADS, T, KV_LORA).transpose(1, 0, 2, 3).reshape(MLA_HEADS, n_tok, KV_LORA)
    mla_o_s = _value_up(o_lat, sgm_s, w["w_uv_pad"])
    y_sample = _merge(x_sample.reshape(1, n_tok, D), ret_o_s.reshape(1, n_tok, RET_WIDTH),
                      mla_o_s.reshape(1, n_tok, MLA_WIDTH), w, min(512, n_tok), False).reshape(NB, T, D)

    return (y_prompt, y_sample, ckv[None], kpe[None], ret_state[None],
            ckv_s[None], kpe_s[None], ret_state_s[None])
```

```python
import functools
import math

import jax
import jax.numpy as jnp
import numpy as np
from jax import lax
from jax.experimental import pallas as pl
from jax.experimental.pallas import tpu as pltpu

F32 = jnp.float32
BF16 = jnp.bfloat16

RET_HEADS = 4
RET_DK = 128
RET_DV = 128
RET_WIDTH = RET_HEADS * RET_DV
RET_CHUNK = 128
MLA_HEADS = 8
MLA_NOPE = 64
MLA_ROPE = 32
MLA_V = 64
MLA_WIDTH = MLA_HEADS * MLA_V
MLA_HEAD_PAD = 128
Q_LORA = 384
KV_LORA = 256
ROPE_BASE = 10000.0
EPS = 1e-6
MLA_Q_SCALE = (MLA_NOPE + MLA_ROPE) ** -0.5 * math.log2(math.e)

LANES = 128
VMEM_LIMIT_BYTES = 56 * 1024 * 1024

_OFF_QR = 0
_OFF_KR = _OFF_QR + RET_WIDTH
_OFF_VR = _OFF_KR + RET_WIDTH
_OFF_GR = _OFF_VR + RET_WIDTH
_OFF_CQ = _OFF_GR + RET_WIDTH
_OFF_KPE = _OFF_CQ + Q_LORA
_OFF_CKV = _OFF_KPE + LANES
_N_PROJ = _OFF_CKV + KV_LORA

_TAB_RC, _TAB_RS, _TAB_MC, _TAB_MS_DOWN, _TAB_MS_UP = (i * LANES for i in range(5))
_TAB_W = 5 * LANES


def _rms(x, g):
    return x * lax.rsqrt(jnp.mean(x * x, axis=-1, keepdims=True) + EPS) * g


def _silu(x):
    return x * jax.nn.sigmoid(x)


def _dot(a, b):
    return jnp.dot(a, b, preferred_element_type=F32)


def _dot_nt(a, b):
    return lax.dot_general(a, b, (((1,), (1,)), ((), ())), preferred_element_type=F32)


def _dot_tn(a, b):
    return lax.dot_general(a, b, (((0,), (0,)), ((), ())), preferred_element_type=F32)


def _proj_kernel(x_ref, tab_ref, ng_ref, wa_ref, wgmt_ref, qg_ref, kvg_ref, wuq_ref, wuk_ref, wuvt_ref, hsel_ref,
                 qr_ref, kr_ref, vr_ref, sgr_ref, sgmt_ref, q_ref, k_ref, vt_ref, ckv_ref, kpe_ref, qn2_ref, kn2_ref):
    h = _rms(x_ref[0], ng_ref[...]).astype(BF16)

    def mm(lo, n):
        return _dot(h, wa_ref[:, lo:lo + n])

    z_lat = mm(_OFF_CQ, Q_LORA + LANES)
    z_ckv = mm(_OFF_CKV, KV_LORA)
    zq = mm(_OFF_QR, RET_WIDTH)
    zk = mm(_OFF_KR, RET_WIDTH)
    zv = mm(_OFF_VR, RET_WIDTH)
    zg = mm(_OFF_GR, RET_WIDTH)
    zgm = _dot_nt(wgmt_ref[...], h)
    cq = _rms(z_lat[:, :Q_LORA], qg_ref[...]).astype(BF16)
    qq = _dot(cq, wuq_ref[...])
    ckv = _rms(z_ckv, kvg_ref[...])
    ckv_b = ckv.astype(BF16)
    k_nope = _dot(ckv_b, wuk_ref[...])
    vt = _dot_nt(wuvt_ref[...], ckv_b)

    rc = tab_ref[:, _TAB_RC:_TAB_RC + LANES]
    rs = tab_ref[:, _TAB_RS:_TAB_RS + LANES]
    mc = tab_ref[:, _TAB_MC:_TAB_MC + LANES]
    ms_down = tab_ref[:, _TAB_MS_DOWN:_TAB_MS_DOWN + LANES]
    ms_up = tab_ref[:, _TAB_MS_UP:_TAB_MS_UP + LANES]

    def ret_rope(z):
        return z * rc + pltpu.roll(z, RET_DK // 2, 1) * rs

    def mla_rope(z):
        return (z * mc + pltpu.roll(z, LANES - MLA_ROPE // 2, 1) * ms_down
                + pltpu.roll(z, MLA_ROPE // 2, 1) * ms_up)

    for i in range(RET_HEADS):
        sl = slice(i * RET_DK, (i + 1) * RET_DK)
        qr_ref[0, :, sl] = ret_rope(zq[:, sl]).astype(qr_ref.dtype)
        kr_ref[0, :, sl] = ret_rope(zk[:, sl]) * (RET_DK ** -0.5)
    vr_ref[0] = zv.astype(vr_ref.dtype)
    sgr_ref[0] = _silu(zg)
    sgmt = _silu(zgm)
    ckv_ref[0] = ckv
    kpe = mla_rope(z_lat[:, Q_LORA:])
    kpe_ref[0] = kpe[:, MLA_NOPE:MLA_NOPE + MLA_ROPE]
    q_sq, k_sq = [], []
    for i in range(MLA_HEADS):
        sl = slice(i * MLA_HEAD_PAD, (i + 1) * MLA_HEAD_PAD)
        sgmt_ref[0, i] = sgmt[i * MLA_V:(i + 1) * MLA_V, :]
        q_head = mla_rope(qq[:, sl]) * MLA_Q_SCALE
        k_head = k_nope[:, sl] + kpe
        q_ref[0, i] = q_head.astype(BF16)
        k_ref[0, i] = k_head.astype(BF16)
        vt_ref[0, i] = vt[i * MLA_V:(i + 1) * MLA_V, :].astype(BF16)
        q_sq.append((q_head * q_head).astype(BF16))
        k_sq.append((k_head * k_head).astype(BF16))
    for sq, out_ref in ((q_sq, qn2_ref), (k_sq, kn2_ref)):
        norms2 = _dot(jnp.concatenate(sq, axis=1), hsel_ref[...])
        out_ref[0, 0] = jnp.broadcast_to(jnp.max(norms2, axis=0, keepdims=True), (8, LANES))


PROJ_TOKENS = 512


def _proj(x3, tab, w, tm, ret_dtype):
    nb, L, D = x3.shape
    tok = lambda n: pl.BlockSpec((1, tm, n), lambda j, b: (b, j, 0))
    head_tok = pl.BlockSpec((1, MLA_HEADS, tm, MLA_HEAD_PAD), lambda j, b: (b, 0, j, 0))
    head_feat = pl.BlockSpec((1, MLA_HEADS, MLA_V, tm), lambda j, b: (b, 0, 0, j))
    full = lambda a: pl.BlockSpec(a.shape, lambda j, b: (0,) * a.ndim)
    sds = jax.ShapeDtypeStruct
    tok_out = lambda n, dt=F32: (tok(n), sds((nb, L, n), dt))
    outs = [tok_out(RET_WIDTH, ret_dtype), tok_out(RET_WIDTH), tok_out(RET_WIDTH, ret_dtype), tok_out(RET_WIDTH),
            (head_feat, sds((nb, MLA_HEADS, MLA_V, L), F32)),
            (head_tok, sds((nb, MLA_HEADS, L, MLA_HEAD_PAD), BF16)),
            (head_tok, sds((nb, MLA_HEADS, L, MLA_HEAD_PAD), BF16)),
            (head_feat, sds((nb, MLA_HEADS, MLA_V, L), BF16)),
            tok_out(KV_LORA), tok_out(MLA_ROPE)]
    norm_spec = pl.BlockSpec((1, 1, 8, LANES), lambda j, b: (b, j, 0, 0))
    outs += [(norm_spec, sds((nb, L // tm, 8, LANES), F32))] * 2
    ws = [w["norm_g"], w["w_a"], w["w_gmt"], w["q_g"], w["kv_g"], w["w_uq"], w["w_uk"], w["w_uvt"], w["head_sel"]]
    return pl.pallas_call(
        _proj_kernel,
        grid=(L // tm, nb),
        in_specs=[tok(D), pl.BlockSpec((tm, _TAB_W), lambda j, b: (j, 0))] + [full(a) for a in ws],
        out_specs=[o[0] for o in outs],
        out_shape=[o[1] for o in outs],
        compiler_params=pltpu.CompilerParams(
            dimension_semantics=("arbitrary", "arbitrary"), vmem_limit_bytes=VMEM_LIMIT_BYTES),
        name="proj",
    )(x3, tab, *ws)


def _ret_kernel(q_ref, k_ref, v_ref, sg_ref, st_in_ref, dec_ref, qd_ref, kd_ref, sd_ref,
                o_ref, st_out_ref, st_scr, pad_scr, *, group, rows, chunk, per_step):
    c = pl.program_id(1)

    @pl.when(c == 0)
    def _():
        st_scr[...] = st_in_ref[...]
        pad_scr[...] = jnp.zeros_like(pad_scr)

    units = [(n, g, i) for n in range(per_step) for g in range(group) for i in range(RET_HEADS)]

    def operand(ref, n, g, sl, slot):
        if rows == chunk:
            return ref[g, n * rows:(n + 1) * rows, sl]
        idx = (g * RET_HEADS + sl.start // RET_DK) * 3 + slot
        pad_scr[idx, 0:rows, :] = ref[g, :, sl].astype(F32)
        return pad_scr[idx]

    def products(n, g, i):
        sl = slice(i * RET_DK, (i + 1) * RET_DK)
        q = operand(q_ref, n, g, sl, 0).astype(BF16)
        k = operand(k_ref, n, g, sl, 1)
        v = operand(v_ref, n, g, sl, 2).astype(BF16)
        state = st_scr[g, i]
        return q, k, v, state, _dot_nt(q, k.astype(BF16)), _dot(q, state.astype(BF16))

    assert RET_LOOKAHEAD < group * RET_HEADS or per_step == 1
    ahead = [products(*u) for u in units[:RET_LOOKAHEAD]]
    for t, (n, g, i) in enumerate(units):
        sl = slice(i * RET_DK, (i + 1) * RET_DK)
        tok = slice(n * rows, (n + 1) * rows)
        q, k, v, state, a, cross = ahead.pop(0)
        if t + RET_LOOKAHEAD < len(units):
            ahead.append(products(*units[t + RET_LOOKAHEAD]))
        o = _dot((a * dec_ref[i]).astype(BF16), v) + cross * qd_ref[i]
        o = o * lax.rsqrt(jnp.mean(o * o, axis=-1, keepdims=True) + EPS)
        o_ref[g, tok, sl] = (o[0:rows] * sg_ref[g, tok, sl]).astype(BF16)
        kd = (k * kd_ref[i]).astype(BF16)
        st_scr[g, i] = sd_ref[i] * state + _dot_tn(kd, v)

    @pl.when(c == pl.num_programs(1) - 1)
    def _():
        st_out_ref[...] = st_scr[...]


RET_CHUNKS_PER_STEP = 2
RET_LOOKAHEAD = 3


def _ret_tables(rows, chunk):
    lg = jnp.log1p(-jnp.exp2(-5.0 - jnp.arange(RET_HEADS, dtype=F32)))
    idx = jnp.arange(rows, dtype=F32)
    diff = idx[:, None] - idx[None, :]
    decay = jnp.where(diff[None] >= 0, jnp.exp(jnp.maximum(diff, 0.0)[None] * lg[:, None, None]), 0.0)
    q_decay = jnp.exp((idx[None, :] + 1.0) * lg[:, None])
    k_decay = jnp.exp((rows - 1.0 - idx)[None, :] * lg[:, None])
    s_decay = jnp.exp(rows * lg)
    pad = chunk - rows
    decay = jnp.pad(decay, ((0, 0), (0, pad), (0, pad)))
    bcast = lambda t: jnp.broadcast_to(jnp.pad(t, ((0, 0), (0, pad)))[:, :, None], (RET_HEADS, chunk, LANES))
    s_decay = jnp.broadcast_to(s_decay[:, None, None], (RET_HEADS, RET_DK, LANES))
    return decay, bcast(q_decay), bcast(k_decay), s_decay


def _retention(q, k, v, sg, state, group, rows):
    nb, L, _ = q.shape
    chunk = max(rows, RET_CHUNK)
    per_step = math.gcd(L // rows, RET_CHUNKS_PER_STEP) if rows == chunk else 1
    dec, qd, kd, sd = _ret_tables(rows, chunk)
    tok = pl.BlockSpec((group, rows * per_step, RET_WIDTH), lambda b, c: (b, c, 0))
    st = pl.BlockSpec((group, RET_HEADS, RET_DK, RET_DV), lambda b, c: (b, 0, 0, 0))
    full = lambda a: pl.BlockSpec(a.shape, lambda b, c: (0,) * a.ndim)
    return pl.pallas_call(
        functools.partial(_ret_kernel, group=group, rows=rows, chunk=chunk, per_step=per_step),
        grid=(nb // group, L // (rows * per_step)),
        in_specs=[tok, tok, tok, tok, st, full(dec), full(qd), full(kd), full(sd)],
        out_specs=[tok, st],
        out_shape=[jax.ShapeDtypeStruct((nb, L, RET_WIDTH), BF16),
                   jax.ShapeDtypeStruct(state.shape, F32)],
        scratch_shapes=[pltpu.VMEM((group, RET_HEADS, RET_DK, RET_DV), F32),
                        pltpu.VMEM((3 * group * RET_HEADS if rows < chunk else 3, chunk, RET_DK), F32)],
        compiler_params=pltpu.CompilerParams(
            dimension_semantics=("arbitrary", "arbitrary"), vmem_limit_bytes=VMEM_LIMIT_BYTES),
        name="retention",
    )(q, k, v, sg, state, dec, qd, kd, sd)


def _flash_kernel(qi_ref, ki_ref, q_ref, k_ref, vt_ref, sgt_ref, qn2_ref, kn2_ref, o_ref,
                  m_scr, l_scr, acc_scr, mmin_scr, *, tile, sub_q, sub_k, lookahead):
    step = pl.program_id(1)
    qi = qi_ref[step]
    ki = ki_ref[step]

    bound2 = (jnp.max(qn2_ref[0], axis=0) * jnp.max(kn2_ref[0], axis=0))[0:1] * FLASH_BOUND_SLACK
    lane = lax.broadcasted_iota(jnp.int32, (1, LANES), 1)

    def no_head_exceeds(limit):
        bad = jnp.where(limit > 0.0, jnp.where(bound2 <= limit * limit, 0.0, 1.0), 1.0)
        return jnp.sum(jnp.where(lane < MLA_HEADS, bad, 0.0)) == 0.0

    @pl.when(ki == 0)
    def _():
        start = jnp.where(no_head_exceeds(jnp.full((1, LANES), FLASH_STALE_MARGIN, F32)), 0.0, -jnp.inf)
        m_scr[...] = jnp.full(m_scr.shape, start, F32)
        mmin_scr[...] = jnp.full(mmin_scr.shape, start, F32)
        l_scr[...] = jnp.zeros_like(l_scr)
        acc_scr[...] = jnp.zeros_like(acc_scr)

    keep_stabiliser = no_head_exceeds(mmin_scr[...] + FLASH_STALE_MARGIN)

    def accumulate(diagonal, stale, heads_per_iter):
        units = [(hh, a, b) for hh in range(heads_per_iter)
                 for a in range(tile // sub_q) for b in range(tile // sub_k)
                 if not diagonal or b * sub_k <= (a + 1) * sub_q - 1]

        def head_group(g, carry):
            def scores(hh, a, b):
                i = g * heads_per_iter + hh
                return _dot_nt(k_ref[0, i, pl.ds(b * sub_k, sub_k), :], q_ref[0, i, pl.ds(a * sub_q, sub_q), :])

            ahead = [scores(*u) for u in units[:lookahead]]
            for t, (hh, a, b) in enumerate(units):
                i = g * heads_per_iter + hh
                s = ahead.pop(0)
                if t + lookahead < len(units):
                    ahead.append(scores(*units[t + lookahead]))
                qs = pl.ds(a * sub_q, sub_q)
                if t == 0 or units[t - 1][:2] != (hh, a):
                    m, l, acc = m_scr[i, :, qs], l_scr[i, :, qs], acc_scr[i, :, qs]
                if diagonal and (b + 1) * sub_k - 1 > a * sub_q:
                    key = b * sub_k + lax.broadcasted_iota(jnp.int32, (sub_k, sub_q), 0)
                    query = a * sub_q + lax.broadcasted_iota(jnp.int32, (sub_k, sub_q), 1)
                    s = jnp.where(key <= query, s, -jnp.inf)
                if not stale:
                    m_next = jnp.maximum(m, jnp.max(s, axis=0, keepdims=True))
                    alpha = jnp.exp2(m - m_next)
                    l, acc, m = alpha * l, alpha * acc, m_next
                p = jnp.exp2(s - m).astype(BF16)
                vt1 = jnp.concatenate([vt_ref[0, i, :, pl.ds(b * sub_k, sub_k)], jnp.ones((16, sub_k), BF16)], axis=0)
                pv = _dot(vt1, p)
                l = l + pv[MLA_V:MLA_V + 1]
                acc = acc + pv[:MLA_V]
                if t + 1 == len(units) or units[t + 1][:2] != (hh, a):
                    l_scr[i, :, qs], acc_scr[i, :, qs] = l, acc
                    if not stale:
                        m_scr[i, :, qs] = m
            return carry

        lax.fori_loop(0, MLA_HEADS // heads_per_iter, head_group, 0)

    for diagonal in (False, True):
        on_tile = (ki == qi) if diagonal else (ki < qi)

        @pl.when(jnp.logical_and(on_tile, keep_stabiliser))
        def _():
            accumulate(diagonal, True, FLASH_HEADS_PER_ITER)

        @pl.when(jnp.logical_and(on_tile, jnp.logical_not(keep_stabiliser)))
        def _():
            accumulate(diagonal, False, FLASH_HEADS_PER_ITER_EXACT)
            smallest = jnp.full((1, LANES), jnp.inf, F32)
            for i in range(MLA_HEADS):
                smallest = jnp.where(lane == i, jnp.min(m_scr[i], axis=1, keepdims=True), smallest)
            mmin_scr[...] = smallest

    @pl.when(ki == qi)
    def _():
        for i in range(MLA_HEADS):
            o_ref[0, i] = (acc_scr[i] / l_scr[i] * sgt_ref[0, i]).astype(BF16)


FLASH_TILE = 2048
FLASH_SUB_Q = 512
FLASH_SUB_K = 512
FLASH_LOOKAHEAD = 2
FLASH_HEADS_PER_ITER = 1
FLASH_HEADS_PER_ITER_EXACT = 1
FLASH_STALE_MARGIN = 60.0
FLASH_BOUND_SLACK = 1.05


def _flash(q, k, vt, sgt, qn2, kn2, tile, sub_q, sub_k, lookahead):
    nb, _, S, _ = q.shape
    nt = S // tile
    norms_per_tile = kn2.shape[1] // nt
    pairs = [(a, b) for a in range(nt) for b in range(a + 1)]
    qi = jnp.asarray(np.array([p[0] for p in pairs], np.int32))
    ki = jnp.asarray(np.array([p[1] for p in pairs], np.int32))
    q_tok = pl.BlockSpec((1, MLA_HEADS, tile, MLA_HEAD_PAD), lambda b, s, qi, ki: (b, 0, qi[s], 0))
    k_tok = pl.BlockSpec((1, MLA_HEADS, tile, MLA_HEAD_PAD), lambda b, s, qi, ki: (b, 0, ki[s], 0))
    k_feat = pl.BlockSpec((1, MLA_HEADS, MLA_V, tile), lambda b, s, qi, ki: (b, 0, 0, ki[s]))
    q_feat = pl.BlockSpec((1, MLA_HEADS, MLA_V, tile), lambda b, s, qi, ki: (b, 0, 0, qi[s]))
    q_norm = pl.BlockSpec((1, norms_per_tile, 8, LANES), lambda b, s, qi, ki: (b, qi[s], 0, 0))
    k_norm = pl.BlockSpec((1, norms_per_tile, 8, LANES), lambda b, s, qi, ki: (b, ki[s], 0, 0))
    return pl.pallas_call(
        functools.partial(_flash_kernel, tile=tile, sub_q=sub_q, sub_k=sub_k, lookahead=lookahead),
        grid_spec=pltpu.PrefetchScalarGridSpec(
            num_scalar_prefetch=2,
            grid=(nb, len(pairs)),
            in_specs=[q_tok, k_tok, k_feat, q_feat, q_norm, k_norm],
            out_specs=q_feat,
            scratch_shapes=[pltpu.VMEM((MLA_HEADS, 1, tile), F32),
                            pltpu.VMEM((MLA_HEADS, 1, tile), F32),
                            pltpu.VMEM((MLA_HEADS, MLA_V, tile), F32),
                            pltpu.VMEM((1, LANES), F32)]),
        out_shape=jax.ShapeDtypeStruct((nb, MLA_HEADS, MLA_V, S), BF16),
        compiler_params=pltpu.CompilerParams(
            dimension_semantics=("arbitrary", "arbitrary"), vmem_limit_bytes=VMEM_LIMIT_BYTES),
        name="flash",
    )(qi, ki, q, k, vt, sgt, qn2, kn2)


_DECODE_SLOTS = 3
DECODE_PAGES = 64
DECODE_CHUNK = 2048
DECODE_LOOKAHEAD = 3


def _decode_kernel(pt_ref, q_ref, knew_ref, ckvnew_ref, lat_hbm, pet_hbm,
                   o_ref, lat_buf, pet_buf, sem, m_scr, l_scr, acc_scr, newlat_scr, newpe_scr,
                   *, pages, n_steps, page, n_new, chunk):
    b = pl.program_id(0)
    j = pl.program_id(1)
    n_total = pl.num_programs(0) * n_steps
    g = b * n_steps + j

    def copies(step, slot):
        out = []
        for p in range(pages):
            pid = pt_ref[step * pages + p]
            out.append(pltpu.make_async_copy(
                lat_hbm.at[pid], lat_buf.at[slot, pl.ds(p * page, page)], sem.at[slot]))
            out.append(pltpu.make_async_copy(pet_hbm.at[pid], pet_buf.at[slot, p], sem.at[slot]))
        return out

    @pl.when(g == 0)
    def _():
        for cp in copies(0, 0) + copies(1, 1):
            cp.start()
        newlat_scr[...] = jnp.zeros_like(newlat_scr)
        newpe_scr[...] = jnp.zeros_like(newpe_scr)

    @pl.when(j == 0)
    def _():
        m_scr[...] = jnp.full_like(m_scr, -jnp.inf)
        l_scr[...] = jnp.zeros_like(l_scr)
        acc_scr[...] = jnp.zeros_like(acc_scr)

    slot = lax.rem(g, _DECODE_SLOTS)
    for cp in copies(g, slot):
        cp.wait()
    gather = copies(jnp.minimum(g + 2, n_total - 1), lax.rem(g + 2, _DECODE_SLOTS))
    n_chunks = pages * page // chunk
    per_chunk = -(-len(gather) // n_chunks)

    q_lat = q_ref[0, :, :KV_LORA]
    q_pe = q_ref[0, :, KV_LORA:]

    def online(carry, s, values):
        m, l, acc = carry
        m_next = jnp.maximum(m, jnp.max(s, axis=1, keepdims=True))
        alpha = jnp.exp2(m - m_next)
        p = jnp.exp2(s - m_next[:, 0:1])
        l = alpha * l + jnp.sum(p, axis=1, keepdims=True)
        acc = jnp.concatenate([alpha, alpha], axis=1) * acc + _dot(p.astype(BF16), values)
        return m_next, l, acc

    def chunk_scores(c):
        lat = lat_buf[slot, pl.ds(c * chunk, chunk), :].astype(BF16)
        first = c * (chunk // page)
        pet = jnp.concatenate([pet_buf[slot, first + p] for p in range(chunk // page)], axis=1).astype(BF16)
        return lat, _dot_nt(q_lat, lat) + _dot(q_pe, pet)

    carry = (m_scr[...], l_scr[...], acc_scr[...])
    ahead = [chunk_scores(c) for c in range(min(DECODE_LOOKAHEAD, n_chunks))]
    for c in range(n_chunks):
        lat, s = ahead.pop(0)
        if c + DECODE_LOOKAHEAD < n_chunks:
            ahead.append(chunk_scores(c + DECODE_LOOKAHEAD))
        for cp in gather[c * per_chunk:(c + 1) * per_chunk]:
            cp.start()
        carry = online(carry, s, lat)
    m_scr[...], l_scr[...], acc_scr[...] = carry

    @pl.when(j == n_steps - 1)
    def _():
        newlat_scr[0:n_new, :] = ckvnew_ref[0]
        newpe_scr[0:n_new, :] = knew_ref[0]
        ckv_new = newlat_scr[...].astype(BF16)
        k_new = newpe_scr[...].astype(BF16)
        s = _dot_nt(q_lat, ckv_new) + _dot_nt(q_pe, k_new)
        t = lax.rem(lax.broadcasted_iota(jnp.int32, s.shape, 0), n_new)
        u = lax.broadcasted_iota(jnp.int32, s.shape, 1)
        m, l, acc = online(carry, jnp.where(u <= t, s, -jnp.inf), ckv_new)
        o_ref[0] = (acc / jnp.concatenate([l, l], axis=1)).astype(BF16)

    @pl.when(g == n_total - 1)
    def _():
        for ahead_steps in (1, 2):
            for cp in copies(g, lax.rem(g + ahead_steps, _DECODE_SLOTS)):
                cp.wait()


def _decode(page_table, q_abs, k_new, ckv_new, cache_lat, cache_pet, pages, chunk):
    nb, rows, width = q_abs.shape
    n_new = rows // MLA_HEADS
    n_pool, page, _ = cache_lat.shape
    n_pages = page_table.shape[1]
    n_steps = n_pages // pages
    assert nb * n_steps >= _DECODE_SLOTS
    tk = pages * page
    per_b = lambda n0, n1: pl.BlockSpec((1, n0, n1), lambda b, j, pt: (b, 0, 0))
    any_spec = pl.BlockSpec(memory_space=pl.ANY)
    return pl.pallas_call(
        functools.partial(_decode_kernel, pages=pages, n_steps=n_steps, page=page, n_new=n_new, chunk=chunk),
        grid_spec=pltpu.PrefetchScalarGridSpec(
            num_scalar_prefetch=1,
            grid=(nb, n_steps),
            in_specs=[per_b(rows, width), per_b(n_new, MLA_ROPE), per_b(n_new, KV_LORA), any_spec, any_spec],
            out_specs=per_b(rows, KV_LORA),
            scratch_shapes=[pltpu.VMEM((_DECODE_SLOTS, tk, KV_LORA), F32),
                            pltpu.VMEM((_DECODE_SLOTS, pages, MLA_ROPE, page), F32),
                            pltpu.SemaphoreType.DMA((_DECODE_SLOTS,)),
                            pltpu.VMEM((rows, LANES), F32),
                            pltpu.VMEM((rows, LANES), F32),
                            pltpu.VMEM((rows, KV_LORA), F32),
                            pltpu.VMEM((LANES, KV_LORA), F32),
                            pltpu.VMEM((LANES, MLA_ROPE), F32)]),
        out_shape=jax.ShapeDtypeStruct((nb, rows, KV_LORA), BF16),
        compiler_params=pltpu.CompilerParams(
            dimension_semantics=("arbitrary", "arbitrary"), vmem_limit_bytes=VMEM_LIMIT_BYTES),
        name="decode",
    )(page_table.reshape(-1), q_abs, k_new, ckv_new, cache_lat, cache_pet)


def _absorb_kernel(q_ref, wabs_ref, o_ref):
    for i in range(MLA_HEADS):
        o_ref[0, i] = _dot(q_ref[0, i], wabs_ref[i]).astype(BF16)


def _absorb(q, w_abs):
    nb, _, L, _ = q.shape
    width = w_abs.shape[-1]
    return pl.pallas_call(
        _absorb_kernel,
        grid=(nb,),
        in_specs=[pl.BlockSpec((1, MLA_HEADS, L, MLA_HEAD_PAD), lambda b: (b, 0, 0, 0)),
                  pl.BlockSpec(w_abs.shape, lambda b: (0, 0, 0))],
        out_specs=pl.BlockSpec((1, MLA_HEADS, L, width), lambda b: (b, 0, 0, 0)),
        out_shape=jax.ShapeDtypeStruct((nb, MLA_HEADS, L, width), BF16),
        compiler_params=pltpu.CompilerParams(dimension_semantics=("arbitrary",), vmem_limit_bytes=VMEM_LIMIT_BYTES),
        name="absorb",
    )(q, w_abs)


def _value_up_kernel(o_ref, sg_ref, wuv_ref, y_ref):
    y = _dot(o_ref[0], wuv_ref[0])
    for i in range(1, MLA_HEADS):
        y = y + _dot(o_ref[i], wuv_ref[i])
    y_ref[...] = (y * sg_ref[...]).astype(BF16)


def _value_up(o_lat, sg, w_uv_pad):
    _, n_tok, _ = o_lat.shape
    full = lambda a: pl.BlockSpec(a.shape, lambda i: (0,) * a.ndim)
    return pl.pallas_call(
        _value_up_kernel,
        grid=(1,),
        in_specs=[full(o_lat), full(sg), full(w_uv_pad)],
        out_specs=pl.BlockSpec((n_tok, MLA_WIDTH), lambda i: (0, 0)),
        out_shape=jax.ShapeDtypeStruct((n_tok, MLA_WIDTH), BF16),
        compiler_params=pltpu.CompilerParams(dimension_semantics=("arbitrary",), vmem_limit_bytes=VMEM_LIMIT_BYTES),
        name="value_up",
    )(o_lat, sg, w_uv_pad)


def _merge_kernel(x_ref, ret_ref, mla_ref, ng_ref, wg_ref, wbr_ref, wbm_ref, wo_ref, fg_ref, y_ref,
                  *, mla_feature_major):
    x = x_ref[0]
    D = x.shape[-1]
    h = _rms(x, ng_ref[...]).astype(BF16)
    gates = jax.nn.sigmoid(_dot(h, wg_ref[...]))
    p_mla = _dot_tn(mla_ref[0], wbm_ref[...]) if mla_feature_major else _dot(mla_ref[0], wbm_ref[...])
    merged = gates[:, :D] * _dot(ret_ref[0], wbr_ref[...]) + gates[:, D:] * p_mla
    y = x + _dot(merged.astype(BF16), wo_ref[...])
    y_ref[0] = _rms(y, fg_ref[...])


def _merge(x3, ret, mla, w, tm, mla_feature_major):
    nb, L, D = x3.shape
    tok = lambda n: pl.BlockSpec((1, tm, n), lambda b, j: (b, j, 0))
    feat = pl.BlockSpec((1, MLA_WIDTH, tm), lambda b, j: (b, 0, j))
    full = lambda a: pl.BlockSpec(a.shape, lambda b, j: (0,) * a.ndim)
    ws = [w["norm_g"], w["w_gate"], w["w_br_ret"], w["w_br_mla"], w["w_out"], w["final_g"]]
    return pl.pallas_call(
        functools.partial(_merge_kernel, mla_feature_major=mla_feature_major),
        grid=(nb, L // tm),
        in_specs=[tok(D), tok(RET_WIDTH), feat if mla_feature_major else tok(MLA_WIDTH)] + [full(a) for a in ws],
        out_specs=tok(D),
        out_shape=jax.ShapeDtypeStruct((nb, L, D), F32),
        compiler_params=pltpu.CompilerParams(
            dimension_semantics=("arbitrary", "arbitrary"), vmem_limit_bytes=VMEM_LIMIT_BYTES),
        name="merge",
    )(x3, ret, mla, *ws)


def _prep_weights(norm_gain, w_in, q_norm_gain, kv_norm_gain, w_uq, w_uk, w_uv,
                  w_branch_ret, w_branch_mla, w_out, final_norm_gain):
    D = w_in.shape[0]
    sizes = (RET_WIDTH, RET_WIDTH, RET_WIDTH, RET_WIDTH, Q_LORA, KV_LORA, MLA_ROPE, MLA_WIDTH, D, D)
    pts = np.cumsum(sizes)[:-1].tolist()
    wq_r, wk_r, wv_r, wg_r, wc_q, wc_kv, wk_pe, wg_m, wmg_r, wmg_m = jnp.split(w_in, pts, axis=1)
    place = lambda t: jnp.pad(t, ((0, 0), (MLA_NOPE, MLA_HEAD_PAD - MLA_NOPE - MLA_ROPE)))
    w_a = jnp.concatenate([wq_r, wk_r, wv_r, wg_r, wc_q, place(wk_pe), wc_kv], axis=1)
    assert w_a.shape[1] == _N_PROJ

    uq = w_uq.reshape(Q_LORA, MLA_HEADS, MLA_NOPE + MLA_ROPE)
    tail = MLA_HEAD_PAD - MLA_NOPE - MLA_ROPE
    w_uq2 = jnp.pad(uq, ((0, 0), (0, 0), (0, tail))).reshape(Q_LORA, -1)
    uk_pad = jnp.pad(w_uk, ((0, 0), (0, 0), (0, MLA_HEAD_PAD - MLA_NOPE))).reshape(KV_LORA, -1)
    uv = w_uv.reshape(KV_LORA, MLA_WIDTH)

    w_abs = jnp.zeros((MLA_HEADS, MLA_HEAD_PAD, KV_LORA + MLA_ROPE), F32)
    w_abs = w_abs.at[:, :MLA_NOPE, :KV_LORA].set(jnp.transpose(w_uk, (1, 2, 0)))
    w_abs = w_abs.at[:, MLA_NOPE:MLA_NOPE + MLA_ROPE, KV_LORA:].set(jnp.eye(MLA_ROPE, dtype=F32)[None])

    head_sel = jnp.repeat(jnp.eye(MLA_HEADS, LANES, dtype=F32), MLA_HEAD_PAD, axis=0)

    w_uv_pad = jnp.zeros((MLA_HEADS, KV_LORA, MLA_WIDTH), F32)
    for i in range(MLA_HEADS):
        w_uv_pad = w_uv_pad.at[i, :, i * MLA_V:(i + 1) * MLA_V].set(w_uv[:, i, :])

    return dict(
        head_sel=head_sel.astype(BF16), w_uv_pad=w_uv_pad.astype(BF16),
        norm_g=norm_gain.reshape(1, D), q_g=q_norm_gain.reshape(1, Q_LORA), kv_g=kv_norm_gain.reshape(1, KV_LORA),
        final_g=final_norm_gain.reshape(1, D),
        w_a=w_a.astype(BF16), w_gmt=wg_m.T.astype(BF16), w_uq=w_uq2.astype(BF16), w_uk=uk_pad.astype(BF16),
        w_uvt=uv.T.astype(BF16), w_abs=w_abs.astype(BF16),
        w_gate=jnp.concatenate([wmg_r, wmg_m], axis=1).astype(BF16),
        w_br_ret=w_branch_ret.astype(BF16), w_br_mla=w_branch_mla.astype(BF16), w_out=w_out.astype(BF16))


ROPE_SPLIT = 64


def _rope_table(pos):
    n = pos.shape[0]
    lane = np.arange(LANES)

    def inv_freq(d):
        return jnp.power(ROPE_BASE, -jnp.arange(d // 2, dtype=F32) * (2.0 / d))

    def cos_sin(freq):
        if n % ROPE_SPLIT:
            ang = pos.astype(F32)[:, None] * freq[None, :]
            return jnp.cos(ang), jnp.sin(ang)
        grid = pos.reshape(n // ROPE_SPLIT, ROPE_SPLIT)
        hi = grid[:, :1].astype(F32) * freq[None, :]
        lo = (grid[:1, :] - grid[:1, :1]).astype(F32).T * freq[None, :]
        ch, sh, cl, sl = jnp.cos(hi)[:, None, :], jnp.sin(hi)[:, None, :], jnp.cos(lo)[None], jnp.sin(lo)[None]
        return (ch * cl - sh * sl).reshape(n, LANES), (sh * cl + ch * sl).reshape(n, LANES)

    half = MLA_ROPE // 2
    ret_cos, ret_sin = cos_sin(jnp.tile(inv_freq(RET_DK), 2))
    ret_sign = np.where(lane < RET_DK // 2, -1.0, 1.0).astype(np.float32)
    in_rope = (lane >= MLA_NOPE) & (lane < MLA_NOPE + MLA_ROPE)
    mla_cos, mla_sin = cos_sin(jnp.where(in_rope, jnp.tile(inv_freq(MLA_ROPE), LANES // half), 0.0))
    cos_keep = (lane < MLA_NOPE + MLA_ROPE).astype(np.float32)
    down = np.where((lane >= MLA_NOPE) & (lane < MLA_NOPE + half), -1.0, 0.0).astype(np.float32)
    up = np.where((lane >= MLA_NOPE + half) & (lane < MLA_NOPE + MLA_ROPE), 1.0, 0.0).astype(np.float32)
    return jnp.concatenate([ret_cos, ret_sin * ret_sign, mla_cos * cos_keep, mla_sin * down, mla_sin * up], axis=1)


def kernel(x_prompt, x_sample, cache_kv_latent, cache_k_rope, state_retention, page_table, norm_gain, w_in,
           q_norm_gain, kv_norm_gain, w_uq, w_uk, w_uv, w_branch_ret, w_branch_mla, w_out, final_norm_gain):
    assert norm_gain.shape[0] == 1, "single-layer kernel"
    B, S, D = x_prompt.shape
    NB, T, _ = x_sample.shape
    page = cache_kv_latent.shape[2]
    n_pages = page_table.shape[1]
    past_len = n_pages * page
    w = _prep_weights(norm_gain[0], w_in[0], q_norm_gain[0], kv_norm_gain[0], w_uq[0], w_uk[0], w_uv[0],
                      w_branch_ret[0], w_branch_mla[0], w_out[0], final_norm_gain)

    tm = min(PROJ_TOKENS, S)
    qr, kr, vr, sgr, sgmt, q, k, vt, ckv, kpe, qn2, kn2 = _proj(x_prompt, _rope_table(jnp.arange(S, dtype=jnp.int32)), w, tm, BF16)
    rows = min(RET_CHUNK, S)
    ret_o, ret_state = _retention(qr, kr, vr, sgr, jnp.zeros((B, RET_HEADS, RET_DK, RET_DV), F32), B, rows)
    tile = min(FLASH_TILE, S)
    mla_t = _flash(q, k, vt, sgmt, qn2, kn2, tile, min(FLASH_SUB_Q, tile), min(FLASH_SUB_K, tile), FLASH_LOOKAHEAD)
    y_prompt = _merge(x_prompt, ret_o, mla_t.reshape(B, MLA_WIDTH, S), w, min(512, S), True)

    n_tok = NB * T
    tms = min(256, n_tok)
    nbs = n_tok // tms
    pos_s = past_len + jnp.arange(T, dtype=jnp.int32)
    tab_s = jnp.tile(_rope_table(pos_s), (tms // T, 1))
    qr, kr, vr, sgr, sgmt, q, _, _, ckv_s, kpe_s, _, _ = _proj(x_sample.reshape(nbs, tms, D), tab_s, w, tms, F32)
    per_seq = lambda t: t.reshape(NB, T, t.shape[-1])
    qr, kr, vr, sgr, ckv_s, kpe_s = map(per_seq, (qr, kr, vr, sgr, ckv_s, kpe_s))
    ret_o_s, ret_state_s = _retention(qr, kr, vr, sgr, state_retention[0], 8, T)
    q_abs = _absorb(q, w["w_abs"])
    q_abs = q_abs.reshape(nbs, MLA_HEADS, tms // T, T, -1).transpose(0, 2, 1, 3, 4).reshape(NB, MLA_HEADS * T, -1)
    sgm_s = sgmt.reshape(nbs, MLA_WIDTH, tms).transpose(0, 2, 1).reshape(n_tok, MLA_WIDTH)
    decode_pages = math.gcd(n_pages, DECODE_PAGES)
    o_lat = _decode(page_table, q_abs, kpe_s, ckv_s,
                    cache_kv_latent[0], jnp.swapaxes(cache_k_rope[0], 1, 2),
                    pages=decode_pages, chunk=min(DECODE_CHUNK, decode_pages * page))
    o_lat = o_lat.reshape(NB, MLA_HEADS, T, KV_LORA).transpose(1, 0, 2, 3).reshape(MLA_HEADS, n_tok, KV_LORA)
    mla_o_s = _value_up(o_lat, sgm_s, w["w_uv_pad"])
    y_sample = _merge(x_sample.reshape(1, n_tok, D), ret_o_s.reshape(1, n_tok, RET_WIDTH),
                      mla_o_s.reshape(1, n_tok, MLA_WIDTH), w, min(512, n_tok), False).reshape(NB, T, D)

    return (y_prompt, y_sample, ckv[None], kpe[None], ret_state[None],
            ckv_s[None], kpe_s[None], ret_state_s[None])
```

```python
import functools
import math

import jax
import jax.numpy as jnp
import numpy as np
from jax import lax
from jax.experimental import pallas as pl
from jax.experimental.pallas import tpu as pltpu

F32 = jnp.float32
BF16 = jnp.bfloat16

RET_HEADS = 4
RET_DK = 128
RET_DV = 128
RET_WIDTH = RET_HEADS * RET_DV
RET_CHUNK = 128
MLA_HEADS = 8
MLA_NOPE = 64
MLA_ROPE = 32
MLA_V = 64
MLA_WIDTH = MLA_HEADS * MLA_V
MLA_HEAD_PAD = 128
Q_LORA = 384
KV_LORA = 256
ROPE_BASE = 10000.0
EPS = 1e-6
MLA_Q_SCALE = (MLA_NOPE + MLA_ROPE) ** -0.5 * math.log2(math.e)

LANES = 128
VMEM_LIMIT_BYTES = 56 * 1024 * 1024

_OFF_QR = 0
_OFF_KR = _OFF_QR + RET_WIDTH
_OFF_VR = _OFF_KR + RET_WIDTH
_OFF_GR = _OFF_VR + RET_WIDTH
_OFF_CQ = _OFF_GR + RET_WIDTH
_OFF_KPE = _OFF_CQ + Q_LORA
_OFF_CKV = _OFF_KPE + LANES
_N_PROJ = _OFF_CKV + KV_LORA

_TAB_RC, _TAB_RS, _TAB_MC, _TAB_MS_DOWN, _TAB_MS_UP = (i * LANES for i in range(5))
_TAB_W = 5 * LANES


def _rms(x, g):
    return x * lax.rsqrt(jnp.mean(x * x, axis=-1, keepdims=True) + EPS) * g


def _silu(x):
    return x * jax.nn.sigmoid(x)


def _dot(a, b):
    return jnp.dot(a, b, preferred_element_type=F32)


def _dot_nt(a, b):
    return lax.dot_general(a, b, (((1,), (1,)), ((), ())), preferred_element_type=F32)


def _dot_tn(a, b):
    return lax.dot_general(a, b, (((0,), (0,)), ((), ())), preferred_element_type=F32)


def _proj_kernel(x_ref, tab_ref, ng_ref, wa_ref, wgmt_ref, qg_ref, kvg_ref, wuq_ref, wuk_ref, wuvt_ref, hsel_ref,
                 qr_ref, kr_ref, vr_ref, sgr_ref, sgmt_ref, q_ref, k_ref, vt_ref, ckv_ref, kpe_ref, qn2_ref, kn2_ref):
    h = _rms(x_ref[0], ng_ref[...]).astype(BF16)

    def mm(lo, n):
        return _dot(h, wa_ref[:, lo:lo + n])

    z_lat = mm(_OFF_CQ, Q_LORA + LANES)
    z_ckv = mm(_OFF_CKV, KV_LORA)
    zq = mm(_OFF_QR, RET_WIDTH)
    zk = mm(_OFF_KR, RET_WIDTH)
    zv = mm(_OFF_VR, RET_WIDTH)
    zg = mm(_OFF_GR, RET_WIDTH)
    zgm = _dot_nt(wgmt_ref[...], h)
    cq = _rms(z_lat[:, :Q_LORA], qg_ref[...]).astype(BF16)
    qq = _dot(cq, wuq_ref[...])
    ckv = _rms(z_ckv, kvg_ref[...])
    ckv_b = ckv.astype(BF16)
    k_nope = _dot(ckv_b, wuk_ref[...])
    vt = _dot_nt(wuvt_ref[...], ckv_b)

    rc = tab_ref[:, _TAB_RC:_TAB_RC + LANES]
    rs = tab_ref[:, _TAB_RS:_TAB_RS + LANES]
    mc = tab_ref[:, _TAB_MC:_TAB_MC + LANES]
    ms_down = tab_ref[:, _TAB_MS_DOWN:_TAB_MS_DOWN + LANES]
    ms_up = tab_ref[:, _TAB_MS_UP:_TAB_MS_UP + LANES]

    def ret_rope(z):
        return z * rc + pltpu.roll(z, RET_DK // 2, 1) * rs

    def mla_rope(z):
        return (z * mc + pltpu.roll(z, LANES - MLA_ROPE // 2, 1) * ms_down
                + pltpu.roll(z, MLA_ROPE // 2, 1) * ms_up)

    for i in range(RET_HEADS):
        sl = slice(i * RET_DK, (i + 1) * RET_DK)
        qr_ref[0, :, sl] = ret_rope(zq[:, sl]).astype(qr_ref.dtype)
        kr_ref[0, :, sl] = ret_rope(zk[:, sl]) * (RET_DK ** -0.5)
    vr_ref[0] = zv.astype(vr_ref.dtype)
    sgr_ref[0] = _silu(zg)
    sgmt = _silu(zgm)
    ckv_ref[0] = ckv
    kpe = mla_rope(z_lat[:, Q_LORA:])
    kpe_ref[0] = kpe[:, MLA_NOPE:MLA_NOPE + MLA_ROPE]
    q_sq, k_sq = [], []
    for i in range(MLA_HEADS):
        sl = slice(i * MLA_HEAD_PAD, (i + 1) * MLA_HEAD_PAD)
        sgmt_ref[0, i] = sgmt[i * MLA_V:(i + 1) * MLA_V, :]
        q_head = mla_rope(qq[:, sl]) * MLA_Q_SCALE
        k_head = k_nope[:, sl] + kpe
        q_ref[0, i] = q_head.astype(BF16)
        k_ref[0, i] = k_head.astype(BF16)
        vt_ref[0, i] = vt[i * MLA_V:(i + 1) * MLA_V, :].astype(BF16)
        q_sq.append((q_head * q_head).astype(BF16))
        k_sq.append((k_head * k_head).astype(BF16))
    for sq, out_ref in ((q_sq, qn2_ref), (k_sq, kn2_ref)):
        norms2 = _dot(jnp.concatenate(sq, axis=1), hsel_ref[...])
        out_ref[0, 0] = jnp.broadcast_to(jnp.max(norms2, axis=0, keepdims=True), (8, LANES))


PROJ_TOKENS = 512


def _proj(x3, tab, w, tm, ret_dtype):
    nb, L, D = x3.shape
    tok = lambda n: pl.BlockSpec((1, tm, n), lambda j, b: (b, j, 0))
    head_tok = pl.BlockSpec((1, MLA_HEADS, tm, MLA_HEAD_PAD), lambda j, b: (b, 0, j, 0))
    head_feat = pl.BlockSpec((1, MLA_HEADS, MLA_V, tm), lambda j, b: (b, 0, 0, j))
    full = lambda a: pl.BlockSpec(a.shape, lambda j, b: (0,) * a.ndim)
    sds = jax.ShapeDtypeStruct
    tok_out = lambda n, dt=F32: (tok(n), sds((nb, L, n), dt))
    outs = [tok_out(RET_WIDTH, ret_dtype), tok_out(RET_WIDTH), tok_out(RET_WIDTH, ret_dtype), tok_out(RET_WIDTH),
            (head_feat, sds((nb, MLA_HEADS, MLA_V, L), F32)),
            (head_tok, sds((nb, MLA_HEADS, L, MLA_HEAD_PAD), BF16)),
            (head_tok, sds((nb, MLA_HEADS, L, MLA_HEAD_PAD), BF16)),
            (head_feat, sds((nb, MLA_HEADS, MLA_V, L), BF16)),
            tok_out(KV_LORA), tok_out(MLA_ROPE)]
    norm_spec = pl.BlockSpec((1, 1, 8, LANES), lambda j, b: (b, j, 0, 0))
    outs += [(norm_spec, sds((nb, L // tm, 8, LANES), F32))] * 2
    ws = [w["norm_g"], w["w_a"], w["w_gmt"], w["q_g"], w["kv_g"], w["w_uq"], w["w_uk"], w["w_uvt"], w["head_sel"]]
    return pl.pallas_call(
        _proj_kernel,
        grid=(L // tm, nb),
        in_specs=[tok(D), pl.BlockSpec((tm, _TAB_W), lambda j, b: (j, 0))] + [full(a) for a in ws],
        out_specs=[o[0] for o in outs],
        out_shape=[o[1] for o in outs],
        compiler_params=pltpu.CompilerParams(
            dimension_semantics=("arbitrary", "arbitrary"), vmem_limit_bytes=VMEM_LIMIT_BYTES),
        name="proj",
    )(x3, tab, *ws)


def _ret_kernel(q_ref, k_ref, v_ref, sg_ref, st_in_ref, dec_ref, qd_ref, kd_ref, sd_ref,
                o_ref, st_out_ref, st_scr, pad_scr, *, group, rows, chunk, per_step):
    c = pl.program_id(1)

    @pl.when(c == 0)
    def _():
        st_scr[...] = st_in_ref[...]
        pad_scr[...] = jnp.zeros_like(pad_scr)

    units = [(n, g, i) for n in range(per_step) for g in range(group) for i in range(RET_HEADS)]

    def operand(ref, n, g, sl, slot):
        if rows == chunk:
            return ref[g, n * rows:(n + 1) * rows, sl]
        idx = (g * RET_HEADS + sl.start // RET_DK) * 3 + slot
        pad_scr[idx, 0:rows, :] = ref[g, :, sl].astype(F32)
        return pad_scr[idx]

    def products(n, g, i):
        sl = slice(i * RET_DK, (i + 1) * RET_DK)
        q = operand(q_ref, n, g, sl, 0).astype(BF16)
        k = operand(k_ref, n, g, sl, 1)
        v = operand(v_ref, n, g, sl, 2).astype(BF16)
        state = st_scr[g, i]
        return q, k, v, state, _dot_nt(q, k.astype(BF16)), _dot(q, state.astype(BF16))

    assert RET_LOOKAHEAD < group * RET_HEADS or per_step == 1
    ahead = [products(*u) for u in units[:RET_LOOKAHEAD]]
    for t, (n, g, i) in enumerate(units):
        sl = slice(i * RET_DK, (i + 1) * RET_DK)
        tok = slice(n * rows, (n + 1) * rows)
        q, k, v, state, a, cross = ahead.pop(0)
        if t + RET_LOOKAHEAD < len(units):
            ahead.append(products(*units[t + RET_LOOKAHEAD]))
        o = _dot((a * dec_ref[i]).astype(BF16), v) + cross * qd_ref[i]
        o = o * lax.rsqrt(jnp.mean(o * o, axis=-1, keepdims=True) + EPS)
        o_ref[g, tok, sl] = (o[0:rows] * sg_ref[g, tok, sl]).astype(BF16)
        kd = (k * kd_ref[i]).astype(BF16)
        st_scr[g, i] = sd_ref[i] * state + _dot_tn(kd, v)

    @pl.when(c == pl.num_programs(1) - 1)
    def _():
        st_out_ref[...] = st_scr[...]


RET_CHUNKS_PER_STEP = 2
RET_LOOKAHEAD = 3


def _ret_tables(rows, chunk):
    lg = jnp.log1p(-jnp.exp2(-5.0 - jnp.arange(RET_HEADS, dtype=F32)))
    idx = jnp.arange(rows, dtype=F32)
    diff = idx[:, None] - idx[None, :]
    decay = jnp.where(diff[None] >= 0, jnp.exp(jnp.maximum(diff, 0.0)[None] * lg[:, None, None]), 0.0)
    q_decay = jnp.exp((idx[None, :] + 1.0) * lg[:, None])
    k_decay = jnp.exp((rows - 1.0 - idx)[None, :] * lg[:, None])
    s_decay = jnp.exp(rows * lg)
    pad = chunk - rows
    decay = jnp.pad(decay, ((0, 0), (0, pad), (0, pad)))
    bcast = lambda t: jnp.broadcast_to(jnp.pad(t, ((0, 0), (0, pad)))[:, :, None], (RET_HEADS, chunk, LANES))
    s_decay = jnp.broadcast_to(s_decay[:, None, None], (RET_HEADS, RET_DK, LANES))
    return decay, bcast(q_decay), bcast(k_decay), s_decay


def _retention(q, k, v, sg, state, group, rows):
    nb, L, _ = q.shape
    chunk = max(rows, RET_CHUNK)
    per_step = math.gcd(L // rows, RET_CHUNKS_PER_STEP) if rows == chunk else 1
    dec, qd, kd, sd = _ret_tables(rows, chunk)
    tok = pl.BlockSpec((group, rows * per_step, RET_WIDTH), lambda b, c: (b, c, 0))
    st = pl.BlockSpec((group, RET_HEADS, RET_DK, RET_DV), lambda b, c: (b, 0, 0, 0))
    full = lambda a: pl.BlockSpec(a.shape, lambda b, c: (0,) * a.ndim)
    return pl.pallas_call(
        functools.partial(_ret_kernel, group=group, rows=rows, chunk=chunk, per_step=per_step),
        grid=(nb // group, L // (rows * per_step)),
        in_specs=[tok, tok, tok, tok, st, full(dec), full(qd), full(kd), full(sd)],
        out_specs=[tok, st],
        out_shape=[jax.ShapeDtypeStruct((nb, L, RET_WIDTH), BF16),
                   jax.ShapeDtypeStruct(state.shape, F32)],
        scratch_shapes=[pltpu.VMEM((group, RET_HEADS, RET_DK, RET_DV), F32),
                        pltpu.VMEM((3 * group * RET_HEADS if rows < chunk else 3, chunk, RET_DK), F32)],
        compiler_params=pltpu.CompilerParams(
            dimension_semantics=("arbitrary", "arbitrary"), vmem_limit_bytes=VMEM_LIMIT_BYTES),
        name="retention",
    )(q, k, v, sg, state, dec, qd, kd, sd)


def _flash_kernel(qi_ref, ki_ref, q_ref, k_ref, vt_ref, sgt_ref, qn2_ref, kn2_ref, o_ref,
                  m_scr, l_scr, acc_scr, mmin_scr, *, tile, sub_q, sub_k, lookahead):
    step = pl.program_id(1)
    qi = qi_ref[step]
    ki = ki_ref[step]

    bound2 = (jnp.max(qn2_ref[0], axis=0) * jnp.max(kn2_ref[0], axis=0))[0:1] * FLASH_BOUND_SLACK
    lane = lax.broadcasted_iota(jnp.int32, (1, LANES), 1)

    def no_head_exceeds(limit):
        bad = jnp.where(limit > 0.0, jnp.where(bound2 <= limit * limit, 0.0, 1.0), 1.0)
        return jnp.sum(jnp.where(lane < MLA_HEADS, bad, 0.0)) == 0.0

    @pl.when(ki == 0)
    def _():
        start = jnp.where(no_head_exceeds(jnp.full((1, LANES), FLASH_STALE_MARGIN, F32)), 0.0, -jnp.inf)
        m_scr[...] = jnp.full(m_scr.shape, start, F32)
        mmin_scr[...] = jnp.full(mmin_scr.shape, start, F32)
        l_scr[...] = jnp.zeros_like(l_scr)
        acc_scr[...] = jnp.zeros_like(acc_scr)

    keep_stabiliser = no_head_exceeds(mmin_scr[...] + FLASH_STALE_MARGIN)

    def accumulate(diagonal, stale, heads_per_iter):
        units = [(hh, a, b) for hh in range(heads_per_iter)
                 for a in range(tile // sub_q) for b in range(tile // sub_k)
                 if not diagonal or b * sub_k <= (a + 1) * sub_q - 1]

        def head_group(g, carry):
            def scores(hh, a, b):
                i = g * heads_per_iter + hh
                return _dot_nt(k_ref[0, i, pl.ds(b * sub_k, sub_k), :], q_ref[0, i, pl.ds(a * sub_q, sub_q), :])

            ahead = [scores(*u) for u in units[:lookahead]]
            for t, (hh, a, b) in enumerate(units):
                i = g * heads_per_iter + hh
                s = ahead.pop(0)
                if t + lookahead < len(units):
                    ahead.append(scores(*units[t + lookahead]))
                qs = pl.ds(a * sub_q, sub_q)
                if t == 0 or units[t - 1][:2] != (hh, a):
                    m, l, acc = m_scr[i, :, qs], l_scr[i, :, qs], acc_scr[i, :, qs]
                if diagonal and (b + 1) * sub_k - 1 > a * sub_q:
                    key = b * sub_k + lax.broadcasted_iota(jnp.int32, (sub_k, sub_q), 0)
                    query = a * sub_q + lax.broadcasted_iota(jnp.int32, (sub_k, sub_q), 1)
                    s = jnp.where(key <= query, s, -jnp.inf)
                if not stale:
                    m_next = jnp.maximum(m, jnp.max(s, axis=0, keepdims=True))
                    alpha = jnp.exp2(m - m_next)
                    l, acc, m = alpha * l, alpha * acc, m_next
                p = jnp.exp2(s - m).astype(BF16)
                vt1 = jnp.concatenate([vt_ref[0, i, :, pl.ds(b * sub_k, sub_k)], jnp.ones((16, sub_k), BF16)], axis=0)
                pv = _dot(vt1, p)
                l = l + pv[MLA_V:MLA_V + 1]
                acc = acc + pv[:MLA_V]
                if t + 1 == len(units) or units[t + 1][:2] != (hh, a):
                    l_scr[i, :, qs], acc_scr[i, :, qs] = l, acc
                    if not stale:
                        m_scr[i, :, qs] = m
            return carry

        lax.fori_loop(0, MLA_HEADS // heads_per_iter, head_group, 0)

    for diagonal in (False, True):
        on_tile = (ki == qi) if diagonal else (ki < qi)

        @pl.when(jnp.logical_and(on_tile, keep_stabiliser))
        def _():
            accumulate(diagonal, True, FLASH_HEADS_PER_ITER)

        @pl.when(jnp.logical_and(on_tile, jnp.logical_not(keep_stabiliser)))
        def _():
            accumulate(diagonal, False, FLASH_HEADS_PER_ITER_EXACT)
            smallest = jnp.full((1, LANES), jnp.inf, F32)
            for i in range(MLA_HEADS):
                smallest = jnp.where(lane == i, jnp.min(m_scr[i], axis=1, keepdims=True), smallest)
            mmin_scr[...] = smallest

    @pl.when(ki == qi)
    def _():
        for i in range(MLA_HEADS):
            o_ref[0, i] = (acc_scr[i] / l_scr[i] * sgt_ref[0, i]).astype(BF16)


FLASH_TILE = 2048
FLASH_SUB_Q = 512
FLASH_SUB_K = 256
FLASH_LOOKAHEAD = 2
FLASH_HEADS_PER_ITER = 1
FLASH_HEADS_PER_ITER_EXACT = 1
FLASH_STALE_MARGIN = 60.0
FLASH_BOUND_SLACK = 1.05


def _flash(q, k, vt, sgt, qn2, kn2, tile, sub_q, sub_k, lookahead):
    nb, _, S, _ = q.shape
    nt = S // tile
    norms_per_tile = kn2.shape[1] // nt
    pairs = [(a, b) for a in range(nt) for b in range(a + 1)]
    qi = jnp.asarray(np.array([p[0] for p in pairs], np.int32))
    ki = jnp.asarray(np.array([p[1] for p in pairs], np.int32))
    q_tok = pl.BlockSpec((1, MLA_HEADS, tile, MLA_HEAD_PAD), lambda b, s, qi, ki: (b, 0, qi[s], 0))
    k_tok = pl.BlockSpec((1, MLA_HEADS, tile, MLA_HEAD_PAD), lambda b, s, qi, ki: (b, 0, ki[s], 0))
    k_feat = pl.BlockSpec((1, MLA_HEADS, MLA_V, tile), lambda b, s, qi, ki: (b, 0, 0, ki[s]))
    q_feat = pl.BlockSpec((1, MLA_HEADS, MLA_V, tile), lambda b, s, qi, ki: (b, 0, 0, qi[s]))
    q_norm = pl.BlockSpec((1, norms_per_tile, 8, LANES), lambda b, s, qi, ki: (b, qi[s], 0, 0))
    k_norm = pl.BlockSpec((1, norms_per_tile, 8, LANES), lambda b, s, qi, ki: (b, ki[s], 0, 0))
    return pl.pallas_call(
        functools.partial(_flash_kernel, tile=tile, sub_q=sub_q, sub_k=sub_k, lookahead=lookahead),
        grid_spec=pltpu.PrefetchScalarGridSpec(
            num_scalar_prefetch=2,
            grid=(nb, len(pairs)),
            in_specs=[q_tok, k_tok, k_feat, q_feat, q_norm, k_norm],
            out_specs=q_feat,
            scratch_shapes=[pltpu.VMEM((MLA_HEADS, 1, tile), F32),
                            pltpu.VMEM((MLA_HEADS, 1, tile), F32),
                            pltpu.VMEM((MLA_HEADS, MLA_V, tile), F32),
                            pltpu.VMEM((1, LANES), F32)]),
        out_shape=jax.ShapeDtypeStruct((nb, MLA_HEADS, MLA_V, S), BF16),
        compiler_params=pltpu.CompilerParams(
            dimension_semantics=("arbitrary", "arbitrary"), vmem_limit_bytes=VMEM_LIMIT_BYTES),
        name="flash",
    )(qi, ki, q, k, vt, sgt, qn2, kn2)


_DECODE_SLOTS = 3
DECODE_PAGES = 64
DECODE_CHUNK = 2048
DECODE_LOOKAHEAD = 3


def _decode_kernel(pt_ref, q_ref, knew_ref, ckvnew_ref, lat_hbm, pet_hbm,
                   o_ref, lat_buf, pet_buf, sem, m_scr, l_scr, acc_scr, newlat_scr, newpe_scr,
                   *, pages, n_steps, page, n_new, chunk):
    b = pl.program_id(0)
    j = pl.program_id(1)
    n_total = pl.num_programs(0) * n_steps
    g = b * n_steps + j

    def copies(step, slot):
        out = []
        for p in range(pages):
            pid = pt_ref[step * pages + p]
            out.append(pltpu.make_async_copy(
                lat_hbm.at[pid], lat_buf.at[slot, pl.ds(p * page, page)], sem.at[slot]))
            out.append(pltpu.make_async_copy(pet_hbm.at[pid], pet_buf.at[slot, p], sem.at[slot]))
        return out

    @pl.when(g == 0)
    def _():
        for cp in copies(0, 0) + copies(1, 1):
            cp.start()
        newlat_scr[...] = jnp.zeros_like(newlat_scr)
        newpe_scr[...] = jnp.zeros_like(newpe_scr)

    @pl.when(j == 0)
    def _():
        m_scr[...] = jnp.full_like(m_scr, -jnp.inf)
        l_scr[...] = jnp.zeros_like(l_scr)
        acc_scr[...] = jnp.zeros_like(acc_scr)

    slot = lax.rem(g, _DECODE_SLOTS)
    for cp in copies(g, slot):
        cp.wait()
    gather = copies(jnp.minimum(g + 2, n_total - 1), lax.rem(g + 2, _DECODE_SLOTS))
    n_chunks = pages * page // chunk
    per_chunk = -(-len(gather) // n_chunks)

    q_lat = q_ref[0, :, :KV_LORA]
    q_pe = q_ref[0, :, KV_LORA:]

    def online(carry, s, values):
        m, l, acc = carry
        m_next = jnp.maximum(m, jnp.max(s, axis=1, keepdims=True))
        alpha = jnp.exp2(m - m_next)
        p = jnp.exp2(s - m_next[:, 0:1])
        l = alpha * l + jnp.sum(p, axis=1, keepdims=True)
        acc = jnp.concatenate([alpha, alpha], axis=1) * acc + _dot(p.astype(BF16), values)
        return m_next, l, acc

    def chunk_scores(c):
        lat = lat_buf[slot, pl.ds(c * chunk, chunk), :].astype(BF16)
        first = c * (chunk // page)
        pet = jnp.concatenate([pet_buf[slot, first + p] for p in range(chunk // page)], axis=1).astype(BF16)
        return lat, _dot_nt(q_lat, lat) + _dot(q_pe, pet)

    carry = (m_scr[...], l_scr[...], acc_scr[...])
    ahead = [chunk_scores(c) for c in range(min(DECODE_LOOKAHEAD, n_chunks))]
    for c in range(n_chunks):
        lat, s = ahead.pop(0)
        if c + DECODE_LOOKAHEAD < n_chunks:
            ahead.append(chunk_scores(c + DECODE_LOOKAHEAD))
        for cp in gather[c * per_chunk:(c + 1) * per_chunk]:
            cp.start()
        carry = online(carry, s, lat)
    m_scr[...], l_scr[...], acc_scr[...] = carry

    @pl.when(j == n_steps - 1)
    def _():
        newlat_scr[0:n_new, :] = ckvnew_ref[0]
        newpe_scr[0:n_new, :] = knew_ref[0]
        ckv_new = newlat_scr[...].astype(BF16)
        k_new = newpe_scr[...].astype(BF16)
        s = _dot_nt(q_lat, ckv_new) + _dot_nt(q_pe, k_new)
        t = lax.rem(lax.broadcasted_iota(jnp.int32, s.shape, 0), n_new)
        u = lax.broadcasted_iota(jnp.int32, s.shape, 1)
        m, l, acc = online(carry, jnp.where(u <= t, s, -jnp.inf), ckv_new)
        o_ref[0] = (acc / jnp.concatenate([l, l], axis=1)).astype(BF16)

    @pl.when(g == n_total - 1)
    def _():
        for ahead_steps in (1, 2):
            for cp in copies(g, lax.rem(g + ahead_steps, _DECODE_SLOTS)):
                cp.wait()


def _decode(page_table, q_abs, k_new, ckv_new, cache_lat, cache_pet, pages, chunk):
    nb, rows, width = q_abs.shape
    n_new = rows // MLA_HEADS
    n_pool, page, _ = cache_lat.shape
    n_pages = page_table.shape[1]
    n_steps = n_pages // pages
    assert nb * n_steps >= _DECODE_SLOTS
    tk = pages * page
    per_b = lambda n0, n1: pl.BlockSpec((1, n0, n1), lambda b, j, pt: (b, 0, 0))
    any_spec = pl.BlockSpec(memory_space=pl.ANY)
    return pl.pallas_call(
        functools.partial(_decode_kernel, pages=pages, n_steps=n_steps, page=page, n_new=n_new, chunk=chunk),
        grid_spec=pltpu.PrefetchScalarGridSpec(
            num_scalar_prefetch=1,
            grid=(nb, n_steps),
            in_specs=[per_b(rows, width), per_b(n_new, MLA_ROPE), per_b(n_new, KV_LORA), any_spec, any_spec],
            out_specs=per_b(rows, KV_LORA),
            scratch_shapes=[pltpu.VMEM((_DECODE_SLOTS, tk, KV_LORA), F32),
                            pltpu.VMEM((_DECODE_SLOTS, pages, MLA_ROPE, page), F32),
                            pltpu.SemaphoreType.DMA((_DECODE_SLOTS,)),
                            pltpu.VMEM((rows, LANES), F32),
                            pltpu.VMEM((rows, LANES), F32),
                            pltpu.VMEM((rows, KV_LORA), F32),
                            pltpu.VMEM((LANES, KV_LORA), F32),
                            pltpu.VMEM((LANES, MLA_ROPE), F32)]),
        out_shape=jax.ShapeDtypeStruct((nb, rows, KV_LORA), BF16),
        compiler_params=pltpu.CompilerParams(
            dimension_semantics=("arbitrary", "arbitrary"), vmem_limit_bytes=VMEM_LIMIT_BYTES),
        name="decode",
    )(page_table.reshape(-1), q_abs, k_new, ckv_new, cache_lat, cache_pet)


def _absorb_kernel(q_ref, wabs_ref, o_ref):
    for i in range(MLA_HEADS):
        o_ref[0, i] = _dot(q_ref[0, i], wabs_ref[i]).astype(BF16)


def _absorb(q, w_abs):
    nb, _, L, _ = q.shape
    width = w_abs.shape[-1]
    return pl.pallas_call(
        _absorb_kernel,
        grid=(nb,),
        in_specs=[pl.BlockSpec((1, MLA_HEADS, L, MLA_HEAD_PAD), lambda b: (b, 0, 0, 0)),
                  pl.BlockSpec(w_abs.shape, lambda b: (0, 0, 0))],
        out_specs=pl.BlockSpec((1, MLA_HEADS, L, width), lambda b: (b, 0, 0, 0)),
        out_shape=jax.ShapeDtypeStruct((nb, MLA_HEADS, L, width), BF16),
        compiler_params=pltpu.CompilerParams(dimension_semantics=("arbitrary",), vmem_limit_bytes=VMEM_LIMIT_BYTES),
        name="absorb",
    )(q, w_abs)


def _value_up_kernel(o_ref, sg_ref, wuv_ref, y_ref):
    y = _dot(o_ref[0], wuv_ref[0])
    for i in range(1, MLA_HEADS):
        y = y + _dot(o_ref[i], wuv_ref[i])
    y_ref[...] = (y * sg_ref[...]).astype(BF16)


def _value_up(o_lat, sg, w_uv_pad):
    _, n_tok, _ = o_lat.shape
    full = lambda a: pl.BlockSpec(a.shape, lambda i: (0,) * a.ndim)
    return pl.pallas_call(
        _value_up_kernel,
        grid=(1,),
        in_specs=[full(o_lat), full(sg), full(w_uv_pad)],
        out_specs=pl.BlockSpec((n_tok, MLA_WIDTH), lambda i: (0, 0)),
        out_shape=jax.ShapeDtypeStruct((n_tok, MLA_WIDTH), BF16),
        compiler_params=pltpu.CompilerParams(dimension_semantics=("arbitrary",), vmem_limit_bytes=VMEM_LIMIT_BYTES),
        name="value_up",
    )(o_lat, sg, w_uv_pad)


def _merge_kernel(x_ref, ret_ref, mla_ref, ng_ref, wg_ref, wbr_ref, wbm_ref, wo_ref, fg_ref, y_ref,
                  *, mla_feature_major):
    x = x_ref[0]
    D = x.shape[-1]
    h = _rms(x, ng_ref[...]).astype(BF16)
    gates = jax.nn.sigmoid(_dot(h, wg_ref[...]))
    p_mla = _dot_tn(mla_ref[0], wbm_ref[...]) if mla_feature_major else _dot(mla_ref[0], wbm_ref[...])
    merged = gates[:, :D] * _dot(ret_ref[0], wbr_ref[...]) + gates[:, D:] * p_mla
    y = x + _dot(merged.astype(BF16), wo_ref[...])
    y_ref[0] = _rms(y, fg_ref[...])


def _merge(x3, ret, mla, w, tm, mla_feature_major):
    nb, L, D = x3.shape
    tok = lambda n: pl.BlockSpec((1, tm, n), lambda b, j: (b, j, 0))
    feat = pl.BlockSpec((1, MLA_WIDTH, tm), lambda b, j: (b, 0, j))
    full = lambda a: pl.BlockSpec(a.shape, lambda b, j: (0,) * a.ndim)
    ws = [w["norm_g"], w["w_gate"], w["w_br_ret"], w["w_br_mla"], w["w_out"], w["final_g"]]
    return pl.pallas_call(
        functools.partial(_merge_kernel, mla_feature_major=mla_feature_major),
        grid=(nb, L // tm),
        in_specs=[tok(D), tok(RET_WIDTH), feat if mla_feature_major else tok(MLA_WIDTH)] + [full(a) for a in ws],
        out_specs=tok(D),
        out_shape=jax.ShapeDtypeStruct((nb, L, D), F32),
        compiler_params=pltpu.CompilerParams(
            dimension_semantics=("arbitrary", "arbitrary"), vmem_limit_bytes=VMEM_LIMIT_BYTES),
        name="merge",
    )(x3, ret, mla, *ws)


def _prep_weights(norm_gain, w_in, q_norm_gain, kv_norm_gain, w_uq, w_uk, w_uv,
                  w_branch_ret, w_branch_mla, w_out, final_norm_gain):
    D = w_in.shape[0]
    sizes = (RET_WIDTH, RET_WIDTH, RET_WIDTH, RET_WIDTH, Q_LORA, KV_LORA, MLA_ROPE, MLA_WIDTH, D, D)
    pts = np.cumsum(sizes)[:-1].tolist()
    wq_r, wk_r, wv_r, wg_r, wc_q, wc_kv, wk_pe, wg_m, wmg_r, wmg_m = jnp.split(w_in, pts, axis=1)
    place = lambda t: jnp.pad(t, ((0, 0), (MLA_NOPE, MLA_HEAD_PAD - MLA_NOPE - MLA_ROPE)))
    w_a = jnp.concatenate([wq_r, wk_r, wv_r, wg_r, wc_q, place(wk_pe), wc_kv], axis=1)
    assert w_a.shape[1] == _N_PROJ

    uq = w_uq.reshape(Q_LORA, MLA_HEADS, MLA_NOPE + MLA_ROPE)
    tail = MLA_HEAD_PAD - MLA_NOPE - MLA_ROPE
    w_uq2 = jnp.pad(uq, ((0, 0), (0, 0), (0, tail))).reshape(Q_LORA, -1)
    uk_pad = jnp.pad(w_uk, ((0, 0), (0, 0), (0, MLA_HEAD_PAD - MLA_NOPE))).reshape(KV_LORA, -1)
    uv = w_uv.reshape(KV_LORA, MLA_WIDTH)

    w_abs = jnp.zeros((MLA_HEADS, MLA_HEAD_PAD, KV_LORA + MLA_ROPE), F32)
    w_abs = w_abs.at[:, :MLA_NOPE, :KV_LORA].set(jnp.transpose(w_uk, (1, 2, 0)))
    w_abs = w_abs.at[:, MLA_NOPE:MLA_NOPE + MLA_ROPE, KV_LORA:].set(jnp.eye(MLA_ROPE, dtype=F32)[None])

    head_sel = jnp.repeat(jnp.eye(MLA_HEADS, LANES, dtype=F32), MLA_HEAD_PAD, axis=0)

    w_uv_pad = jnp.zeros((MLA_HEADS, KV_LORA, MLA_WIDTH), F32)
    for i in range(MLA_HEADS):
        w_uv_pad = w_uv_pad.at[i, :, i * MLA_V:(i + 1) * MLA_V].set(w_uv[:, i, :])

    return dict(
        head_sel=head_sel.astype(BF16), w_uv_pad=w_uv_pad.astype(BF16),
        norm_g=norm_gain.reshape(1, D), q_g=q_norm_gain.reshape(1, Q_LORA), kv_g=kv_norm_gain.reshape(1, KV_LORA),
        final_g=final_norm_gain.reshape(1, D),
        w_a=w_a.astype(BF16), w_gmt=wg_m.T.astype(BF16), w_uq=w_uq2.astype(BF16), w_uk=uk_pad.astype(BF16),
        w_uvt=uv.T.astype(BF16), w_abs=w_abs.astype(BF16),
        w_gate=jnp.concatenate([wmg_r, wmg_m], axis=1).astype(BF16),
        w_br_ret=w_branch_ret.astype(BF16), w_br_mla=w_branch_mla.astype(BF16), w_out=w_out.astype(BF16))


ROPE_SPLIT = 64


def _rope_table(pos):
    n = pos.shape[0]
    lane = np.arange(LANES)

    def inv_freq(d):
        return jnp.power(ROPE_BASE, -jnp.arange(d // 2, dtype=F32) * (2.0 / d))

    def cos_sin(freq):
        if n % ROPE_SPLIT:
            ang = pos.astype(F32)[:, None] * freq[None, :]
            return jnp.cos(ang), jnp.sin(ang)
        grid = pos.reshape(n // ROPE_SPLIT, ROPE_SPLIT)
        hi = grid[:, :1].astype(F32) * freq[None, :]
        lo = (grid[:1, :] - grid[:1, :1]).astype(F32).T * freq[None, :]
        ch, sh, cl, sl = jnp.cos(hi)[:, None, :], jnp.sin(hi)[:, None, :], jnp.cos(lo)[None], jnp.sin(lo)[None]
        return (ch * cl - sh * sl).reshape(n, LANES), (sh * cl + ch * sl).reshape(n, LANES)

    half = MLA_ROPE // 2
    ret_cos, ret_sin = cos_sin(jnp.tile(inv_freq(RET_DK), 2))
    ret_sign = np.where(lane < RET_DK // 2, -1.0, 1.0).astype(np.float32)
    in_rope = (lane >= MLA_NOPE) & (lane < MLA_NOPE + MLA_ROPE)
    mla_cos, mla_sin = cos_sin(jnp.where(in_rope, jnp.tile(inv_freq(MLA_ROPE), LANES // half), 0.0))
    cos_keep = (lane < MLA_NOPE + MLA_ROPE).astype(np.float32)
    down = np.where((lane >= MLA_NOPE) & (lane < MLA_NOPE + half), -1.0, 0.0).astype(np.float32)
    up = np.where((lane >= MLA_NOPE + half) & (lane < MLA_NOPE + MLA_ROPE), 1.0, 0.0).astype(np.float32)
    return jnp.concatenate([ret_cos, ret_sin * ret_sign, mla_cos * cos_keep, mla_sin * down, mla_sin * up], axis=1)


def kernel(x_prompt, x_sample, cache_kv_latent, cache_k_rope, state_retention, page_table, norm_gain, w_in,
           q_norm_gain, kv_norm_gain, w_uq, w_uk, w_uv, w_branch_ret, w_branch_mla, w_out, final_norm_gain):
    assert norm_gain.shape[0] == 1, "single-layer kernel"
    B, S, D = x_prompt.shape
    NB, T, _ = x_sample.shape
    page = cache_kv_latent.shape[2]
    n_pages = page_table.shape[1]
    past_len = n_pages * page
    w = _prep_weights(norm_gain[0], w_in[0], q_norm_gain[0], kv_norm_gain[0], w_uq[0], w_uk[0], w_uv[0],
                      w_branch_ret[0], w_branch_mla[0], w_out[0], final_norm_gain)

    tm = min(PROJ_TOKENS, S)
    qr, kr, vr, sgr, sgmt, q, k, vt, ckv, kpe, qn2, kn2 = _proj(x_prompt, _rope_table(jnp.arange(S, dtype=jnp.int32)), w, tm, BF16)
    rows = min(RET_CHUNK, S)
    ret_o, ret_state = _retention(qr, kr, vr, sgr, jnp.zeros((B, RET_HEADS, RET_DK, RET_DV), F32), B, rows)
    tile = min(FLASH_TILE, S)
    mla_t = _flash(q, k, vt, sgmt, qn2, kn2, tile, min(FLASH_SUB_Q, tile), min(FLASH_SUB_K, tile), FLASH_LOOKAHEAD)
    y_prompt = _merge(x_prompt, ret_o, mla_t.reshape(B, MLA_WIDTH, S), w, min(512, S), True)

    n_tok = NB * T
    tms = min(256, n_tok)
    nbs = n_tok // tms
    pos_s = past_len + jnp.arange(T, dtype=jnp.int32)
    tab_s = jnp.tile(_rope_table(pos_s), (tms // T, 1))
    qr, kr, vr, sgr, sgmt, q, _, _, ckv_s, kpe_s, _, _ = _proj(x_sample.reshape(nbs, tms, D), tab_s, w, tms, F32)
    per_seq = lambda t: t.reshape(NB, T, t.shape[-1])
    qr, kr, vr, sgr, ckv_s, kpe_s = map(per_seq, (qr, kr, vr, sgr, ckv_s, kpe_s))
    ret_o_s, ret_state_s = _retention(qr, kr, vr, sgr, state_retention[0], 8, T)
    q_abs = _absorb(q, w["w_abs"])
    q_abs = q_abs.reshape(nbs, MLA_HEADS, tms // T, T, -1).transpose(0, 2, 1, 3, 4).reshape(NB, MLA_HEADS * T, -1)
    sgm_s = sgmt.reshape(nbs, MLA_WIDTH, tms).transpose(0, 2, 1).reshape(n_tok, MLA_WIDTH)
    decode_pages = math.gcd(n_pages, DECODE_PAGES)
    o_lat = _decode(page_table, q_abs, kpe_s, ckv_s,
                    cache_kv_latent[0], jnp.swapaxes(cache_k_rope[0], 1, 2),
                    pages=decode_pages, chunk=min(DECODE_CHUNK, decode_pages * page))
    o_lat = o_lat.reshape(NB, MLA_HEADS, T, KV_LORA).transpose(1, 0, 2, 3).reshape(MLA_HEADS, n_tok, KV_LORA)
    mla_o_s = _value_up(o_lat, sgm_s, w["w_uv_pad"])
    y_sample = _merge(x_sample.reshape(1, n_tok, D), ret_o_s.reshape(1, n_tok, RET_WIDTH),
                      mla_o_s.reshape(1, n_tok, MLA_WIDTH), w, min(512, n_tok), False).reshape(NB, T, D)

    return (y_prompt, y_sample, ckv[None], kpe[None], ret_state[None],
            ckv_s[None], kpe_s[None], ret_state_s[None])
```
